```python
import math
import jax, jax.numpy as jnp
from jax import lax
import numpy as np

D_MODEL = 4096
BATCH = 4
SEQ = 2048
DEPTH = 2
DEC_BATCH = 1
DEC_SEQ = 8192
PAST_LEN = 128

HEAD_DIM = 128
RET_HEADS = 16
ATT_Q_HEADS = 16
ATT_KV_HEADS = 4
GQA_GROUP = ATT_Q_HEADS // ATT_KV_HEADS
RET_W = RET_HEADS * HEAD_DIM
ATT_Q_W = ATT_Q_HEADS * HEAD_DIM
ATT_KV_W = ATT_KV_HEADS * HEAD_DIM
MIX_W = RET_W + ATT_Q_W
IN_PROJ_W = 4 * RET_W + ATT_Q_W + 2 * ATT_KV_W
SPLITS = (RET_W, 2 * RET_W, 3 * RET_W, 4 * RET_W, 4 * RET_W + ATT_Q_W, 4 * RET_W + ATT_Q_W + ATT_KV_W)
RET_CHUNK = 128
Q_BLOCK = 128
GRID_W = 64
ROPE_THETA = 10000.0
MEM_TOKENS = 256
MEM_HEADS = 4
MEM_HEAD_DIM = 128
MEM_W = MEM_HEADS * MEM_HEAD_DIM
D_FF = 11008
CONV_W = 3
EPS = 1e-6

kernel_name = "hybrid_retention_gqa_encoder"

F32 = jnp.float32


def rms_norm(x, g):
    xf = x.astype(F32)
    y = xf * lax.rsqrt(jnp.mean(xf * xf, axis=-1, keepdims=True) + EPS)
    return (y * g.astype(F32)).astype(x.dtype)


def axial_rope_tables(seq_len):
    rows = seq_len // GRID_W
    r, c = jnp.meshgrid(jnp.arange(rows), jnp.arange(GRID_W), indexing="ij")
    r = r.reshape(-1).astype(F32)
    c = c.reshape(-1).astype(F32)
    half = HEAD_DIM // 2
    freqs = ROPE_THETA ** (-jnp.arange(0, half, 2, dtype=F32) / half)
    ang_r = r[:, None] * freqs
    ang_c = c[:, None] * freqs
    return (jnp.cos(ang_r), jnp.sin(ang_r), jnp.cos(ang_c), jnp.sin(ang_c))


def _rotate(x, cos, sin):
    x1, x2 = jnp.split(x, 2, axis=-1)
    cos = cos[:, None, :]
    sin = sin[:, None, :]
    return jnp.concatenate([x1 * cos - x2 * sin, x2 * cos + x1 * sin], axis=-1)


def apply_axial_rope(x, tables):
    cos_r, sin_r, cos_c, sin_c = tables
    xf = x.astype(F32)
    half = HEAD_DIM // 2
    out = jnp.concatenate([_rotate(xf[..., :half], cos_r, sin_r),
                           _rotate(xf[..., half:], cos_c, sin_c)], axis=-1)
    return out.astype(x.dtype)


def retention_one_direction(q, k, v, log_g, strict):
    B, H, S, dk = q.shape
    dv = v.shape[-1]
    nc = S // RET_CHUNK
    qc = q.reshape(B, H, nc, RET_CHUNK, dk)
    kc = k.reshape(B, H, nc, RET_CHUNK, dk)
    vc = v.reshape(B, H, nc, RET_CHUNK, dv)
    idx = jnp.arange(RET_CHUNK, dtype=F32)
    diff = idx[:, None] - idx[None, :]
    mask = (diff > 0) if strict else (diff >= 0)
    decay = jnp.where(mask, jnp.exp(jnp.maximum(diff, 0.0) * log_g[:, None, None]), 0.0)
    scores = jnp.einsum("bhncd,bhnmd->bhncm", qc, kc) * decay[None, :, None]
    intra = jnp.einsum("bhncm,bhnme->bhnce", scores, vc)
    zeta = jnp.exp((RET_CHUNK - 1 - idx)[None, :] * log_g[:, None])
    kv = jnp.einsum("bhnmd,hm,bhnme->nbhde", kc, zeta, vc)
    chunk_decay = jnp.exp(RET_CHUNK * log_g)[None, :, None, None]

    def step(state, kv_n):
        return chunk_decay * state + kv_n, state

    _, prev = lax.scan(step, jnp.zeros((B, H, dk, dv), F32), kv)
    xi = jnp.exp((idx + 1)[None, :] * log_g[:, None])
    inter = jnp.einsum("bhncd,hc,nbhde->bhnce", qc, xi, prev)
    return (intra + inter).reshape(B, H, S, dv)


def bidirectional_retention(q, k, v, g, log_g):
    B, S, H, dh = v.shape
    qf = q.astype(F32).transpose(0, 2, 1, 3)
    kf = (k.astype(F32) * HEAD_DIM ** -0.5).transpose(0, 2, 1, 3)
    vf = v.astype(F32).transpose(0, 2, 1, 3)
    fw = retention_one_direction(qf, kf, vf, log_g[0], False)
    bw = retention_one_direction(qf[:, :, ::-1], kf[:, :, ::-1], vf[:, :, ::-1], log_g[1], True)[:, :, ::-1]
    y = fw + bw
    y = y * lax.rsqrt(jnp.mean(y * y, axis=-1, keepdims=True) + EPS)
    y = y.transpose(0, 2, 1, 3)
    out = jax.nn.silu(g.astype(F32)) * y
    return out.reshape(B, S, H * dh).astype(v.dtype)


def blocked_gqa(q, k, v):
    B, S, _, dh = q.shape
    nb = S // Q_BLOCK
    qb = q.reshape(B, nb, Q_BLOCK, ATT_KV_HEADS, GQA_GROUP, dh).transpose(1, 0, 3, 4, 2, 5)
    scale = dh ** -0.5

    def one_block(qi):
        s = jnp.einsum("bkgqd,bskd->bkgqs", qi, k, preferred_element_type=F32) * scale
        p = jax.nn.softmax(s, axis=-1)
        return jnp.einsum("bkgqs,bskd->bkgqd", p.astype(v.dtype), v)

    o = lax.map(one_block, qb)
    return o.transpose(1, 0, 4, 2, 3, 5).reshape(B, S, ATT_Q_HEADS * dh)


def memory_cross_attention(xn, memn, w_cq, w_ckv, w_co):
    B, S, _ = xn.shape
    M = memn.shape[1]
    q = (xn @ w_cq).reshape(B, S, MEM_HEADS, MEM_HEAD_DIM)
    kv = (memn @ w_ckv).reshape(B, M, 2, MEM_HEADS, MEM_HEAD_DIM)
    k, v = kv[:, :, 0], kv[:, :, 1]
    s = jnp.einsum("bshd,bmhd->bhsm", q, k, preferred_element_type=F32) * MEM_HEAD_DIM ** -0.5
    p = jax.nn.softmax(s, axis=-1)
    o = jnp.einsum("bhsm,bmhd->bshd", p.astype(v.dtype), v).reshape(B, S, MEM_W)
    return o @ w_co


def conv_gated_ffn(xn, w_up, conv_w, conv_b, w_down):
    h = xn @ w_up
    hp = jnp.pad(h, ((0, 0), (1, 1), (0, 0)))
    c = hp[:, :-2] * conv_w[0] + hp[:, 1:-1] * conv_w[1] + hp[:, 2:] * conv_w[2] + conv_b
    a, u = jnp.split(c, 2, axis=-1)
    return (jax.nn.gelu(a, approximate=False) * u) @ w_down


def encoder_trunk(x, mem, w_in, w_out, ret_decay_logit, q_norm, k_norm, norm_mix, norm_cross,
                  norm_mem, w_cq, w_ckv, w_co, norm_ffn, w_up, conv_w, conv_b, w_down, norm_final):
    B, S, _ = x.shape
    rope = axial_rope_tables(S)
    for l in range(DEPTH):
        n = rms_norm(x, norm_mix[l])
        proj = n @ w_in[l]
        rq, rk, rv, rg, aq, ak, av = jnp.split(proj, SPLITS, axis=-1)
        rq = apply_axial_rope(rq.reshape(B, S, RET_HEADS, HEAD_DIM), rope)
        rk = apply_axial_rope(rk.reshape(B, S, RET_HEADS, HEAD_DIM), rope)
        rv = rv.reshape(B, S, RET_HEADS, HEAD_DIM)
        rg = rg.reshape(B, S, RET_HEADS, HEAD_DIM)
        log_g = jax.nn.log_sigmoid(ret_decay_logit[l].astype(F32))
        ret_out = bidirectional_retention(rq, rk, rv, rg, log_g)
        aq = rms_norm(aq.reshape(B, S, ATT_Q_HEADS, HEAD_DIM), q_norm[l])
        ak = rms_norm(ak.reshape(B, S, ATT_KV_HEADS, HEAD_DIM), k_norm[l])
        aq = apply_axial_rope(aq, rope)
        ak = apply_axial_rope(ak, rope)
        av = av.reshape(B, S, ATT_KV_HEADS, HEAD_DIM)
        att_out = blocked_gqa(aq, ak, av)
        x = x + jnp.concatenate([ret_out, att_out], axis=-1) @ w_out[l]
        memn = rms_norm(mem, norm_mem[l])
        x = x + memory_cross_attention(rms_norm(x, norm_cross[l]), memn, w_cq[l], w_ckv[l], w_co[l])
        x = x + conv_gated_ffn(rms_norm(x, norm_ffn[l]), w_up[l], conv_w[l], conv_b[l], w_down[l])
    return rms_norm(x, norm_final)


def setup_inputs(seed: int = 0) -> dict:
    key = jax.random.key(seed)
    ks = jax.random.split(key, 24)
    nrm = jax.random.normal
    h = jnp.arange(RET_HEADS, dtype=F32)
    base_logit = jnp.log(2.0 ** (5.0 + h) - 1.0)
    return {
        "x_prompt": nrm(ks[0], (BATCH, SEQ, D_MODEL), F32),
        "x_sample": nrm(ks[1], (DEC_BATCH, DEC_SEQ, D_MODEL), F32),
        "mem_prompt": nrm(ks[2], (BATCH, MEM_TOKENS, D_MODEL), F32),
        "mem_sample": nrm(ks[3], (DEC_BATCH, MEM_TOKENS, D_MODEL), F32),
        "w_in": nrm(ks[4], (DEPTH, D_MODEL, IN_PROJ_W), F32) * D_MODEL ** -0.5,
        "w_out": nrm(ks[5], (DEPTH, MIX_W, D_MODEL), F32) * MIX_W ** -0.5,
        "ret_decay_logit": base_logit[None, None, :] + 0.05 * nrm(ks[6], (DEPTH, 2, RET_HEADS), F32),
        "q_norm": 1.0 + 0.02 * nrm(ks[7], (DEPTH, HEAD_DIM), F32),
        "k_norm": 1.0 + 0.02 * nrm(ks[8], (DEPTH, HEAD_DIM), F32),
        "norm_mix": 1.0 + 0.02 * nrm(ks[9], (DEPTH, D_MODEL), F32),
        "norm_cross": 1.0 + 0.02 * nrm(ks[10], (DEPTH, D_MODEL), F32),
        "norm_mem": 1.0 + 0.02 * nrm(ks[11], (DEPTH, D_MODEL), F32),
        "w_cq": nrm(ks[12], (DEPTH, D_MODEL, MEM_W), F32) * D_MODEL ** -0.5,
        "w_ckv": nrm(ks[13], (DEPTH, D_MODEL, 2 * MEM_W), F32) * D_MODEL ** -0.5,
        "w_co": nrm(ks[14], (DEPTH, MEM_W, D_MODEL), F32) * MEM_W ** -0.5,
        "norm_ffn": 1.0 + 0.02 * nrm(ks[15], (DEPTH, D_MODEL), F32),
        "w_up": nrm(ks[16], (DEPTH, D_MODEL, 2 * D_FF), F32) * D_MODEL ** -0.5,
        "conv_w": nrm(ks[17], (DEPTH, CONV_W, 2 * D_FF), F32) * CONV_W ** -0.5,
        "conv_b": 0.02 * nrm(ks[18], (DEPTH, 2 * D_FF), F32),
        "w_down": nrm(ks[19], (DEPTH, D_FF, D_MODEL), F32) * D_FF ** -0.5,
        "norm_final": 1.0 + 0.02 * nrm(ks[20], (D_MODEL,), F32),
    }


def reference(x_prompt, x_sample, mem_prompt, mem_sample, w_in, w_out, ret_decay_logit, q_norm, k_norm,
              norm_mix, norm_cross, norm_mem, w_cq, w_ckv, w_co, norm_ffn, w_up, conv_w, conv_b, w_down,
              norm_final):
    y_prompt = encoder_trunk(x_prompt, mem_prompt, w_in, w_out, ret_decay_logit, q_norm, k_norm, norm_mix,
                             norm_cross, norm_mem, w_cq, w_ckv, w_co, norm_ffn, w_up, conv_w, conv_b, w_down,
                             norm_final)
    y_sample = encoder_trunk(x_sample, mem_sample, w_in, w_out, ret_decay_logit, q_norm, k_norm, norm_mix,
                             norm_cross, norm_mem, w_cq, w_ckv, w_co, norm_ffn, w_up, conv_w, conv_b, w_down,
                             norm_final)
    return (y_prompt, y_sample)
```

```python
import functools

import jax
import jax.numpy as jnp
from jax import lax
from jax.experimental import pallas as pl
from jax.experimental.pallas import tpu as pltpu

F32 = jnp.float32
BF16 = jnp.bfloat16

HEAD_DIM = 128
RET_HEADS = 16
ATT_Q_HEADS = 16
ATT_KV_HEADS = 4
RET_CHUNK = 128
GRID_W = 64
ROPE_THETA = 10000.0
MEM_HEADS = 4
MEM_HEAD_DIM = 128
EPS = 1e-6

V7X_VMEM_BYTES = 64 * 1024 * 1024
V7X_VMEM_CAP = 60000 * 1024
LANES = 128
BF16_SUBLANES = 16

ROW_TILE = 1024
COL_TILE = 1024
NORM_ROWS = 256
CROSS_ROWS = 256
GQA_Q_ROWS = 256
GQA_KV_ROWS = 512
FFN_ROWS = 512
FFN_COLS = 256
HALO = BF16_SUBLANES


def _tile(n, pref):
    if n <= pref:
        return n
    t = pref
    while n % t:
        t //= 2
    return t


def _params(sem, vmem_bytes):
    limit = min(int(vmem_bytes), V7X_VMEM_CAP)
    return pltpu.CompilerParams(dimension_semantics=sem, vmem_limit_bytes=limit)


def _rms(x, g):
    y = x * lax.rsqrt(jnp.mean(x * x, axis=-1, keepdims=True) + EPS)
    return y * g


def _rmsnorm_kernel(x_ref, g_ref, o_ref):
    o_ref[...] = _rms(x_ref[...], g_ref[...]).astype(o_ref.dtype)


def _rmsnorm(x, g, out_dtype, row0=0, rows=None):
    d = x.shape[1]
    rows = x.shape[0] if rows is None else rows
    tm = _tile(rows, NORM_ROWS)
    assert row0 % tm == 0
    blk0 = row0 // tm
    nbytes = 2 * tm * d * (4 + jnp.dtype(out_dtype).itemsize) + 3 * tm * d * 4
    return pl.pallas_call(
        _rmsnorm_kernel,
        grid=(rows // tm,),
        in_specs=[pl.BlockSpec((tm, d), lambda i: (blk0 + i, 0)),
                  pl.BlockSpec((1, d), lambda i: (0, 0))],
        out_specs=pl.BlockSpec((tm, d), lambda i: (i, 0)),
        out_shape=jax.ShapeDtypeStruct((rows, d), out_dtype),
        compiler_params=_params(("parallel",), nbytes),
        name="rmsnorm",
    )(x, g.reshape(1, d))


def _rope(a, cos, sin, low_half):
    partner = jnp.where(low_half, pltpu.roll(a, 96, 1), pltpu.roll(a, 32, 1))
    return a * cos + partner * sin


def _low_half_mask(rows):
    lane = lax.broadcasted_iota(jnp.int32, (rows, HEAD_DIM), 1)
    return (lane & 32) == 0


def _proj_kernel(n_ref, w_ref, *rest, epilogue):
    acc = jnp.dot(n_ref[...], w_ref[...], preferred_element_type=F32)
    epilogue(acc, *rest)


def _epi_plain(acc, o_ref):
    o_ref[...] = acc.astype(o_ref.dtype)


def _epi_silu(acc, o_ref):
    o_ref[...] = jax.nn.silu(acc).astype(o_ref.dtype)


def _epi_ret_qk(acc, cos_ref, sin_ref, o_ref, *, k_block0, k_scale):
    j = pl.program_id(0)
    scale = jnp.where(j >= k_block0, k_scale, 1.0).astype(F32)
    cos, sin = cos_ref[...], sin_ref[...]
    low = _low_half_mask(acc.shape[0])
    for h in range(acc.shape[1] // HEAD_DIM):
        sl = slice(h * HEAD_DIM, (h + 1) * HEAD_DIM)
        o_ref[:, sl] = (_rope(acc[:, sl], cos, sin, low) * scale).astype(o_ref.dtype)


def _norm_rope_heads(acc, cos, sin, g, o_ref):
    low = _low_half_mask(acc.shape[0])
    for h in range(acc.shape[1] // HEAD_DIM):
        sl = slice(h * HEAD_DIM, (h + 1) * HEAD_DIM)
        o_ref[:, sl] = _rope(_rms(acc[:, sl], g), cos, sin, low).astype(o_ref.dtype)


def _epi_att_q(acc, cos_ref, sin_ref, g_ref, o_ref):
    _norm_rope_heads(acc, cos_ref[...], sin_ref[...], g_ref[...], o_ref)


def _epi_att_kv(acc, cos_ref, sin_ref, g_ref, o_ref):
    j = pl.program_id(0)

    @pl.when(j == 0)
    def _():
        _norm_rope_heads(acc, cos_ref[...], sin_ref[...], g_ref[...], o_ref)

    @pl.when(j != 0)
    def _():
        o_ref[...] = acc.astype(o_ref.dtype)


def _proj(n, w, col0, width, tn, out_dtype, epilogue, extras=(), tm_pref=None):
    r, d = n.shape
    tm = _tile(r, ROW_TILE if tm_pref is None else tm_pref)
    assert col0 % tn == 0 and width % tn == 0
    cb0 = col0 // tn
    in_specs = [pl.BlockSpec((tm, d), lambda j, i: (i, 0)),
                pl.BlockSpec((d, tn), lambda j, i: (0, cb0 + j))]
    in_specs += [pl.BlockSpec(shape, imap) for _, shape, imap in extras]
    osize = jnp.dtype(out_dtype).itemsize
    nbytes = (2 * (tm * d * 2 + d * tn * 2 + tm * tn * osize) + 4 * tm * tn * 4
              + sum(2 * 4 * shape[0] * shape[1] for _, shape, _ in extras))
    return pl.pallas_call(
        functools.partial(_proj_kernel, epilogue=epilogue),
        grid=(width // tn, r // tm),
        in_specs=in_specs,
        out_specs=pl.BlockSpec((tm, tn), lambda j, i: (i, j)),
        out_shape=jax.ShapeDtypeStruct((r, width), out_dtype),
        compiler_params=_params(("parallel", "parallel"), nbytes),
        name="proj",
    )(n, w, *[a for a, _, _ in extras])


def _retention_kernel(lg_ref, q_ref, k_ref, v_ref, g_ref, o_ref, bw_ref, *, nc):
    c_len = RET_CHUNK
    h = pl.program_id(1)
    lgf = lg_ref[0, h]
    lgb = lg_ref[1, h]
    ii = lax.broadcasted_iota(jnp.int32, (c_len, c_len), 0).astype(F32)
    jj = lax.broadcasted_iota(jnp.int32, (c_len, c_len), 1).astype(F32)
    diff = ii - jj
    decay = jnp.where(diff >= 0, jnp.exp(jnp.maximum(diff, 0.0) * lgf),
                      jnp.exp(jnp.maximum(-diff, 0.0) * lgb))
    ci = lax.broadcasted_iota(jnp.int32, (c_len, 1), 0).astype(F32)
    xi_f = jnp.exp((ci + 1.0) * lgf)
    zeta_f = jnp.exp((c_len - 1.0 - ci) * lgf)
    xi_b = jnp.exp((c_len - ci) * lgb)
    zeta_b = jnp.exp(ci * lgb)
    one = jnp.ones((1, 1), F32)
    chunk_f = jnp.exp(one * (c_len * lgf))
    chunk_b = jnp.exp(one * (c_len * lgb))
    tn_dims = (((0,), (0,)), ((), ()))
    nt_dims = (((1,), (1,)), ((), ()))

    def rows_of(c):
        return pl.ds(pl.multiple_of(c * c_len, c_len), c_len)

    def bw_body(t, state):
        rows = rows_of(nc - 1 - t)
        q = q_ref[rows, :].astype(F32)
        k = k_ref[rows, :].astype(F32)
        v = v_ref[rows, :]
        bw_ref[rows, :] = jnp.dot((q * xi_b).astype(BF16), state.astype(BF16),
                                  preferred_element_type=F32)
        kv = lax.dot_general((k * zeta_b).astype(BF16), v, tn_dims, preferred_element_type=F32)
        return state * chunk_b + kv

    lax.fori_loop(0, nc, bw_body, jnp.zeros((HEAD_DIM, HEAD_DIM), F32))

    def fw_body(c, state):
        rows = rows_of(c)
        qb = q_ref[rows, :]
        kb = k_ref[rows, :]
        v = v_ref[rows, :]
        q = qb.astype(F32)
        k = kb.astype(F32)
        s = lax.dot_general(qb, kb, nt_dims, preferred_element_type=F32) * decay
        intra = jnp.dot(s.astype(BF16), v, preferred_element_type=F32)
        inter = jnp.dot((q * xi_f).astype(BF16), state.astype(BF16), preferred_element_type=F32)
        y = (intra + inter) + bw_ref[rows, :]
        y = y * lax.rsqrt(jnp.mean(y * y, axis=-1, keepdims=True) + EPS)
        o_ref[rows, :] = (g_ref[rows, :] * y).astype(o_ref.dtype)
        kv = lax.dot_general((k * zeta_f).astype(BF16), v, tn_dims, preferred_element_type=F32)
        return state * chunk_f + kv

    lax.fori_loop(0, nc, fw_body, jnp.zeros((HEAD_DIM, HEAD_DIM), F32))


def _retention(qk, v, gate, log_g, row0, batch, seq):
    assert row0 % seq == 0 and seq % RET_CHUNK == 0
    sb0 = row0 // seq
    hd = HEAD_DIM
    nbytes = 2 * seq * hd * (2 + 2 + 2 + 4 + 2) + seq * hd * 4 + 64 * hd * hd * 4
    return pl.pallas_call(
        functools.partial(_retention_kernel, nc=seq // RET_CHUNK),
        grid=(batch, RET_HEADS),
        in_specs=[pl.BlockSpec(memory_space=pltpu.SMEM),
                  pl.BlockSpec((seq, hd), lambda b, h: (sb0 + b, h)),
                  pl.BlockSpec((seq, hd), lambda b, h: (sb0 + b, RET_HEADS + h)),
                  pl.BlockSpec((seq, hd), lambda b, h: (sb0 + b, h)),
                  pl.BlockSpec((seq, hd), lambda b, h: (sb0 + b, h))],
        out_specs=pl.BlockSpec((seq, hd), lambda b, h: (b, h)),
        out_shape=jax.ShapeDtypeStruct((batch * seq, RET_HEADS * hd), BF16),
        scratch_shapes=[pltpu.VMEM((seq, hd), F32)],
        compiler_params=_params(("parallel", "parallel"), nbytes),
        name="retention",
    )(log_g, qk, qk, v, gate)


def _gqa_kernel(q_ref, k_ref, v_ref, o_ref, qs_ref, m_ref, l_ref, acc_ref, *, nk, tk, tq, group):
    hd = HEAD_DIM
    for g in range(group):
        qs_ref[g * tq:(g + 1) * tq, :] = q_ref[:, g * hd:(g + 1) * hd]
    m_ref[...] = jnp.full(m_ref.shape, -jnp.inf, F32)
    l_ref[...] = jnp.zeros(l_ref.shape, F32)
    acc_ref[...] = jnp.zeros(acc_ref.shape, F32)
    scale = hd ** -0.5
    nt_dims = (((1,), (1,)), ((), ()))

    def body(c, carry):
        rows = pl.ds(pl.multiple_of(c * tk, tk), tk)
        s = lax.dot_general(qs_ref[...], k_ref[rows, :], nt_dims, preferred_element_type=F32) * scale
        m_prev = m_ref[...]
        m_new = jnp.maximum(m_prev, jnp.max(s, axis=-1, keepdims=True))
        alpha = jnp.exp(m_prev - m_new)
        p = jnp.exp(s - m_new)
        l_ref[...] = alpha * l_ref[...] + jnp.sum(p, axis=-1, keepdims=True)
        acc_ref[...] = alpha * acc_ref[...] + jnp.dot(p.astype(BF16), v_ref[rows, :],
                                                      preferred_element_type=F32)
        m_ref[...] = m_new
        return carry

    lax.fori_loop(0, nk, body, 0)
    out = acc_ref[...] / l_ref[...]
    for g in range(group):
        o_ref[:, g * hd:(g + 1) * hd] = out[g * tq:(g + 1) * tq, :].astype(o_ref.dtype)


def _gqa(aq, akv, row0, batch, seq):
    hd = HEAD_DIM
    group = ATT_Q_HEADS // ATT_KV_HEADS
    tq = _tile(seq, GQA_Q_ROWS)
    tk = _tile(seq, GQA_KV_ROWS)
    assert row0 % seq == 0
    sb0 = row0 // seq
    qb0 = row0 // tq
    nq = seq // tq
    gw = group * hd
    nbytes = (2 * (2 * tq * gw * 2 + 2 * seq * hd * 2) + group * tq * hd * (2 + 4)
              + 2 * group * tq * LANES * 4 + 4 * group * tq * tk * 4)
    return pl.pallas_call(
        functools.partial(_gqa_kernel, nk=seq // tk, tk=tk, tq=tq, group=group),
        grid=(batch, ATT_KV_HEADS, nq),
        in_specs=[pl.BlockSpec((tq, gw), lambda b, h, i: (qb0 + b * nq + i, h)),
                  pl.BlockSpec((seq, hd), lambda b, h, i: (sb0 + b, h)),
                  pl.BlockSpec((seq, hd), lambda b, h, i: (sb0 + b, ATT_KV_HEADS + h))],
        out_specs=pl.BlockSpec((tq, gw), lambda b, h, i: (b * nq + i, h)),
        out_shape=jax.ShapeDtypeStruct((batch * seq, ATT_Q_HEADS * hd), BF16),
        scratch_shapes=[pltpu.VMEM((group * tq, hd), BF16),
                        pltpu.VMEM((group * tq, 1), F32),
                        pltpu.VMEM((group * tq, 1), F32),
                        pltpu.VMEM((group * tq, hd), F32)],
        compiler_params=_params(("parallel", "parallel", "parallel"), nbytes),
        name="gqa",
    )(aq, akv, akv)


def _out_proj_kernel(ret_ref, att_ref, w_ref, x_ref, o_ref, *, rw):
    acc = jnp.dot(ret_ref[...], w_ref[:rw, :], preferred_element_type=F32)
    acc = acc + jnp.dot(att_ref[...], w_ref[rw:, :], preferred_element_type=F32)
    o_ref[...] = x_ref[...] + acc


def _out_proj(ret, att, w, x):
    t, d = x.shape
    rw, aw = ret.shape[1], att.shape[1]
    tm = _tile(t, ROW_TILE)
    tn = _tile(d, COL_TILE)
    nbytes = 2 * (tm * (rw + aw) * 2 + (rw + aw) * tn * 2 + 2 * tm * tn * 4) + 3 * tm * tn * 4
    return pl.pallas_call(
        functools.partial(_out_proj_kernel, rw=rw),
        grid=(d // tn, t // tm),
        in_specs=[pl.BlockSpec((tm, rw), lambda j, i: (i, 0)),
                  pl.BlockSpec((tm, aw), lambda j, i: (i, 0)),
                  pl.BlockSpec((rw + aw, tn), lambda j, i: (0, j)),
                  pl.BlockSpec((tm, tn), lambda j, i: (i, j))],
        out_specs=pl.BlockSpec((tm, tn), lambda j, i: (i, j)),
        out_shape=jax.ShapeDtypeStruct((t, d), F32),
        compiler_params=_params(("parallel", "parallel"), nbytes),
        name="out_proj",
    )(ret, att, w, x)


def _cross_kernel(x_ref, kv_ref, wq_ref, wo_ref, gc_ref, gf_ref, x_out_ref, n_out_ref):
    hd = MEM_HEAD_DIM
    mw = MEM_HEADS * hd
    x = x_ref[...]
    n = _rms(x, gc_ref[...]).astype(BF16)
    q = jnp.dot(n, wq_ref[...], preferred_element_type=F32).astype(BF16)
    scale = hd ** -0.5
    nt_dims = (((1,), (1,)), ((), ()))
    heads = []
    for h in range(MEM_HEADS):
        k = kv_ref[:, h * hd:(h + 1) * hd]
        v = kv_ref[:, mw + h * hd:mw + (h + 1) * hd]
        s = lax.dot_general(q[:, h * hd:(h + 1) * hd], k, nt_dims, preferred_element_type=F32) * scale
        p = jnp.exp(s - jnp.max(s, axis=-1, keepdims=True))
        p = p / jnp.sum(p, axis=-1, keepdims=True)
        heads.append(jnp.dot(p.astype(BF16), v, preferred_element_type=F32).astype(BF16))
    o = jnp.concatenate(heads, axis=-1)
    x2 = x + jnp.dot(o, wo_ref[...], preferred_element_type=F32)
    x_out_ref[...] = x2
    n_out_ref[...] = _rms(x2, gf_ref[...]).astype(n_out_ref.dtype)


def _cross(x, kv, wq, wo, g_cross, g_ffn, mem_block_of_tile, tm):
    t, d = x.shape
    m = kv.shape[0] // mem_block_of_tile.n_seq
    mw = wq.shape[1]
    nbytes = (2 * (tm * d * (4 + 4 + 2) + m * 2 * mw * 2 + 2 * d * mw * 2 + 2 * d * 4)
              + 6 * tm * d * 4)
    return pl.pallas_call(
        _cross_kernel,
        grid=(t // tm,),
        in_specs=[pl.BlockSpec((tm, d), lambda i: (i, 0)),
                  pl.BlockSpec((m, 2 * mw), lambda i: (mem_block_of_tile(i), 0)),
                  pl.BlockSpec((d, mw), lambda i: (0, 0)),
                  pl.BlockSpec((mw, d), lambda i: (0, 0)),
                  pl.BlockSpec((1, d), lambda i: (0, 0)),
                  pl.BlockSpec((1, d), lambda i: (0, 0))],
        out_specs=[pl.BlockSpec((tm, d), lambda i: (i, 0)),
                   pl.BlockSpec((tm, d), lambda i: (i, 0))],
        out_shape=[jax.ShapeDtypeStruct((t, d), F32), jax.ShapeDtypeStruct((t, d), BF16)],
        compiler_params=_params(("parallel",), nbytes),
        name="cross",
    )(x, kv, wq, wo, g_cross.reshape(1, d), g_ffn.reshape(1, d))


class _SeqLayout:
    def __init__(self, n_a, len_a, n_b, len_b):
        self.n_a, self.len_a, self.n_b, self.len_b = n_a, len_a, n_b, len_b
        self.rows_a = n_a * len_a
        self.n_seq = n_a + n_b

    def seq_of_row(self, row):
        return jnp.where(row < self.rows_a, row // self.len_a,
                         self.n_a + (row - self.rows_a) // self.len_b)

    def pos_in_seq(self, row):
        return jnp.where(row < self.rows_a, row % self.len_a, (row - self.rows_a) % self.len_b)

    def seq_len_at(self, row):
        return jnp.where(row < self.rows_a, self.len_a, self.len_b)


class _MemBlockOfTile:
    def __init__(self, layout, tm):
        self.layout, self.tm, self.n_seq = layout, tm, layout.n_seq

    def __call__(self, i):
        return self.layout.seq_of_row(i * self.tm)


def _gelu(a):
    return 0.5 * a * (1.0 + lax.erf(a * (2.0 ** -0.5)))


def _ffn_kernel(n_ref, prev_ref, next_ref, wa_ref, wu_ref, cwa_ref, cwu_ref, cba_ref, cbu_ref,
                wd_ref, x_ref, o_ref, lhs_ref, h_ref, *, layout, tm, tf):
    i = pl.program_id(0)
    f = pl.program_id(1)

    @pl.when(f == 0)
    def _():
        row0 = i * tm
        pos = layout.pos_in_seq(row0)
        first = pos == 0
        last = pos + tm == layout.seq_len_at(row0)
        zero = jnp.zeros(prev_ref.shape, prev_ref.dtype)
        lhs_ref[0:HALO, :] = jnp.where(first, zero, prev_ref[...])
        lhs_ref[HALO:HALO + tm, :] = n_ref[...]
        lhs_ref[HALO + tm:, :] = jnp.where(last, zero, next_ref[...])
        o_ref[...] = x_ref[...]

    lhs = lhs_ref[...]
    h_ref[:, :tf] = jnp.dot(lhs, wa_ref[...], preferred_element_type=F32)
    h_ref[:, tf:] = jnp.dot(lhs, wu_ref[...], preferred_element_type=F32)

    def conv(cols, cw_ref, cb_ref):
        lo = h_ref[pl.ds(HALO - 1, tm), cols]
        mid = h_ref[pl.ds(HALO, tm), cols]
        hi = h_ref[pl.ds(HALO + 1, tm), cols]
        return lo * cw_ref[0:1, :] + mid * cw_ref[1:2, :] + hi * cw_ref[2:3, :] + cb_ref[...]

    a = conv(slice(0, tf), cwa_ref, cba_ref)
    u = conv(slice(tf, 2 * tf), cwu_ref, cbu_ref)
    act = (_gelu(a) * u).astype(BF16)
    o_ref[...] += jnp.dot(act, wd_ref[...], preferred_element_type=F32)


def _ffn(n, x, w_up, conv_w, conv_b, w_down, layout):
    t, d = x.shape
    ff = w_down.shape[0]
    tm = _tile(min(layout.len_a, layout.len_b), FFN_ROWS)
    tf = _tile(ff, FFN_COLS)
    assert layout.len_a % tm == 0 and layout.len_b % tm == 0 and tm % HALO == 0
    nf = ff // tf
    hb = tm // HALO
    last_hb = t // HALO - 1
    conv_b = conv_b.reshape(1, 2 * ff)
    nbytes = (2 * (tm * d * 2 + 2 * HALO * d * 2 + 2 * d * tf * 2 + tf * d * 2 + tm * d * 4)
              + tm * d * 4 + (tm + 2 * HALO) * (d * 2 + 2 * tf * 4) + 8 * tm * tf * 4 + tm * d * 4)
    return pl.pallas_call(
        functools.partial(_ffn_kernel, layout=layout, tm=tm, tf=tf),
        grid=(t // tm, nf),
        in_specs=[pl.BlockSpec((tm, d), lambda i, f: (i, 0)),
                  pl.BlockSpec((HALO, d), lambda i, f: (jnp.maximum(i * hb - 1, 0), 0)),
                  pl.BlockSpec((HALO, d), lambda i, f: (jnp.minimum((i + 1) * hb, last_hb), 0)),
                  pl.BlockSpec((d, tf), lambda i, f: (0, f)),
                  pl.BlockSpec((d, tf), lambda i, f: (0, nf + f)),
                  pl.BlockSpec((3, tf), lambda i, f: (0, f)),
                  pl.BlockSpec((3, tf), lambda i, f: (0, nf + f)),
                  pl.BlockSpec((1, tf), lambda i, f: (0, f)),
                  pl.BlockSpec((1, tf), lambda i, f: (0, nf + f)),
                  pl.BlockSpec((tf, d), lambda i, f: (f, 0)),
                  pl.BlockSpec((tm, d), lambda i, f: (i, 0), pipeline_mode=pl.Buffered(1))],
        out_specs=pl.BlockSpec((tm, d), lambda i, f: (i, 0)),
        out_shape=jax.ShapeDtypeStruct((t, d), F32),
        scratch_shapes=[pltpu.VMEM((tm + 2 * HALO, d), BF16),
                        pltpu.VMEM((tm + 2 * HALO, 2 * tf), F32)],
        compiler_params=_params(("parallel", "arbitrary"), nbytes),
        name="ffn",
    )(n, n, n, w_up, w_up, conv_w, conv_w, conv_b, conv_b, w_down, x)


def _rope_tables(layout):
    pos = jnp.concatenate([jnp.tile(jnp.arange(layout.len_a), layout.n_a),
                           jnp.tile(jnp.arange(layout.len_b), layout.n_b)])
    r = (pos // GRID_W).astype(F32)
    c = (pos % GRID_W).astype(F32)
    half = HEAD_DIM // 2
    freqs = ROPE_THETA ** (-jnp.arange(0, half, 2, dtype=F32) / half)
    ang_r = r[:, None] * freqs
    ang_c = c[:, None] * freqs
    cos = jnp.concatenate([jnp.cos(ang_r)] * 2 + [jnp.cos(ang_c)] * 2, axis=-1)
    sin = jnp.concatenate([-jnp.sin(ang_r), jnp.sin(ang_r), -jnp.sin(ang_c), jnp.sin(ang_c)], axis=-1)
    return cos, sin


def kernel(x_prompt, x_sample, mem_prompt, mem_sample, w_in, w_out, ret_decay_logit, q_norm, k_norm, norm_mix, norm_cross, norm_mem, w_cq, w_ckv, w_co, norm_ffn, w_up, conv_w, conv_b, w_down, norm_final):
    b, s, d = x_prompt.shape
    db, ds, _ = x_sample.shape
    m = mem_prompt.shape[1]
    depth = w_in.shape[0]
    layout = _SeqLayout(b, s, db, ds)
    t1, t2 = b * s, db * ds
    t = t1 + t2
    rw = RET_HEADS * HEAD_DIM
    aqw = ATT_Q_HEADS * HEAD_DIM
    akw = ATT_KV_HEADS * HEAD_DIM
    assert w_in.shape[2] == 4 * rw + aqw + 2 * akw and rw + aqw == w_out.shape[1]

    x = jnp.concatenate([x_prompt.reshape(t1, d), x_sample.reshape(t2, d)], axis=0)
    mem = jnp.concatenate([mem_prompt.reshape(b * m, d), mem_sample.reshape(db * m, d)], axis=0)
    cos, sin = _rope_tables(layout)
    log_g = jax.nn.log_sigmoid(ret_decay_logit.astype(F32))

    tm = _tile(t, ROW_TILE)
    tn = _tile(rw, COL_TILE)
    rope_extras = [(cos, (tm, HEAD_DIM), lambda j, i: (i, 0)),
                   (sin, (tm, HEAD_DIM), lambda j, i: (i, 0))]
    cross_tm = _tile(min(s, ds), CROSS_ROWS)
    mem_block = _MemBlockOfTile(layout, cross_tm)

    n = _rmsnorm(x, norm_mix[0], BF16)
    for l in range(depth):
        wi = w_in[l].astype(BF16)
        qk = _proj(n, wi, 0, 2 * rw, tn, BF16,
                   functools.partial(_epi_ret_qk, k_block0=rw // tn, k_scale=HEAD_DIM ** -0.5),
                   rope_extras)
        rv = _proj(n, wi, 2 * rw, rw, tn, BF16, _epi_plain)
        gate = _proj(n, wi, 3 * rw, rw, tn, F32, _epi_silu)
        aq = _proj(n, wi, 4 * rw, aqw, _tile(aqw, COL_TILE), BF16, _epi_att_q,
                   rope_extras + [(q_norm[l].reshape(1, HEAD_DIM), (1, HEAD_DIM), lambda j, i: (0, 0))])
        akv = _proj(n, wi, 4 * rw + aqw, 2 * akw, akw, BF16, _epi_att_kv,
                    rope_extras + [(k_norm[l].reshape(1, HEAD_DIM), (1, HEAD_DIM), lambda j, i: (0, 0))])
        ret = jnp.concatenate([_retention(qk, rv, gate, log_g[l], 0, b, s),
                               _retention(qk, rv, gate, log_g[l], t1, db, ds)], axis=0)
        att = jnp.concatenate([_gqa(aq, akv, 0, b, s), _gqa(aq, akv, t1, db, ds)], axis=0)
        x = _out_proj(ret, att, w_out[l].astype(BF16), x)

        memn = _rmsnorm(mem, norm_mem[l], BF16)
        kv = _proj(memn, w_ckv[l].astype(BF16), 0, w_ckv.shape[2], _tile(w_ckv.shape[2], COL_TILE),
                   BF16, _epi_plain, tm_pref=NORM_ROWS)
        x, n = _cross(x, kv, w_cq[l].astype(BF16), w_co[l].astype(BF16), norm_cross[l], norm_ffn[l],
                      mem_block, cross_tm)
        x = _ffn(n, x, w_up[l].astype(BF16), conv_w[l], conv_b[l], w_down[l].astype(BF16), layout)
        if l + 1 < depth:
            n = _rmsnorm(x, norm_mix[l + 1], BF16)

    y_prompt = _rmsnorm(x, norm_final, F32, 0, t1).reshape(b, s, d)
    y_sample = _rmsnorm(x, norm_final, F32, t1, t2).reshape(db, ds, d)
    return (y_prompt, y_sample)
```

```python
import functools

import jax
import jax.numpy as jnp
from jax import lax
from jax.experimental import pallas as pl
from jax.experimental.pallas import tpu as pltpu

F32 = jnp.float32
BF16 = jnp.bfloat16

HEAD_DIM = 128
RET_HEADS = 16
ATT_Q_HEADS = 16
ATT_KV_HEADS = 4
RET_CHUNK = 128
GRID_W = 64
ROPE_THETA = 10000.0
MEM_HEADS = 4
MEM_HEAD_DIM = 128
EPS = 1e-6
LOG2E = 1.4426950408889634

V7X_VMEM_BYTES = 64 * 1024 * 1024
V7X_VMEM_CAP = 60000 * 1024
LANES = 128
BF16_SUBLANES = 16

ROW_TILE = 1024
COL_TILE = 1024
NORM_ROWS = 256
CROSS_ROWS = 256
GQA_Q_ROWS = 512
GQA_KV_ROWS = 1024
GQA_COL_BLOCK = 512
RET_UNROLL = 8
FFN_ROWS = 512
FFN_COLS = 256
HALO = BF16_SUBLANES


def _tile(n, pref):
    if n <= pref:
        return n
    t = pref
    while n % t:
        t //= 2
    return t


def _params(sem, vmem_bytes):
    limit = min(int(vmem_bytes), V7X_VMEM_CAP)
    return pltpu.CompilerParams(dimension_semantics=sem, vmem_limit_bytes=limit)


def _rms(x, g):
    y = x * lax.rsqrt(jnp.mean(x * x, axis=-1, keepdims=True) + EPS)
    return y * g


def _rmsnorm_kernel(x_ref, g_ref, o_ref):
    o_ref[...] = _rms(x_ref[...], g_ref[...]).astype(o_ref.dtype)


def _rmsnorm(x, g, out_dtype, row0=0, rows=None):
    d = x.shape[1]
    rows = x.shape[0] if rows is None else rows
    tm = _tile(rows, NORM_ROWS)
    assert row0 % tm == 0
    blk0 = row0 // tm
    nbytes = 2 * tm * d * (4 + jnp.dtype(out_dtype).itemsize) + 3 * tm * d * 4
    return pl.pallas_call(
        _rmsnorm_kernel,
        grid=(rows // tm,),
        in_specs=[pl.BlockSpec((tm, d), lambda i: (blk0 + i, 0)),
                  pl.BlockSpec((1, d), lambda i: (0, 0))],
        out_specs=pl.BlockSpec((tm, d), lambda i: (i, 0)),
        out_shape=jax.ShapeDtypeStruct((rows, d), out_dtype),
        compiler_params=_params(("parallel",), nbytes),
        name="rmsnorm",
    )(x, g.reshape(1, d))


def _rope(a, cos, sin, low_half):
    partner = jnp.where(low_half, pltpu.roll(a, 96, 1), pltpu.roll(a, 32, 1))
    return a * cos + partner * sin


def _low_half_mask(rows):
    lane = lax.broadcasted_iota(jnp.int32, (rows, HEAD_DIM), 1)
    return (lane & 32) == 0


def _proj_kernel(n_ref, w_ref, *rest, epilogue):
    acc = jnp.dot(n_ref[...], w_ref[...], preferred_element_type=F32)
    epilogue(acc, *rest)


def _epi_plain(acc, o_ref):
    o_ref[...] = acc.astype(o_ref.dtype)


def _epi_silu(acc, o_ref):
    o_ref[...] = jax.nn.silu(acc).astype(o_ref.dtype)


def _epi_ret_qk(acc, cos_ref, sin_ref, o_ref, *, k_block0, k_scale):
    j = pl.program_id(0)
    scale = jnp.where(j >= k_block0, k_scale, 1.0).astype(F32)
    cos, sin = cos_ref[...], sin_ref[...]
    low = _low_half_mask(acc.shape[0])
    for h in range(acc.shape[1] // HEAD_DIM):
        sl = slice(h * HEAD_DIM, (h + 1) * HEAD_DIM)
        o_ref[:, sl] = (_rope(acc[:, sl], cos, sin, low) * scale).astype(o_ref.dtype)


def _norm_rope_heads(acc, cos, sin, g, o_ref, scale=None):
    low = _low_half_mask(acc.shape[0])
    for h in range(acc.shape[1] // HEAD_DIM):
        sl = slice(h * HEAD_DIM, (h + 1) * HEAD_DIM)
        y = _rope(_rms(acc[:, sl], g), cos, sin, low)
        o_ref[:, sl] = (y if scale is None else y * scale).astype(o_ref.dtype)


def _epi_att_q(acc, cos_ref, sin_ref, g_ref, o_ref):
    _norm_rope_heads(acc, cos_ref[...], sin_ref[...], g_ref[...], o_ref, scale=HEAD_DIM ** -0.5 * LOG2E)


def _epi_att_k(acc, cos_ref, sin_ref, g_ref, o_ref):
    _norm_rope_heads(acc, cos_ref[...], sin_ref[...], g_ref[...], o_ref)


def _epi_transposed_chunks(acc, o_ref):
    tk = o_ref.shape[2]
    for c in range(o_ref.shape[0]):
        o_ref[c] = acc[c * tk:(c + 1) * tk, :].T.astype(o_ref.dtype)


def _proj(n, w, layer, col0, width, tn, out_dtype, epilogue, extras=(), tm_pref=None,
          transposed_chunk=None):
    r, d = n.shape
    tm = _tile(r, ROW_TILE if tm_pref is None else tm_pref)
    assert col0 % tn == 0 and width % tn == 0
    cb0 = col0 // tn
    in_specs = [pl.BlockSpec((tm, d), lambda j, i: (i, 0)),
                pl.BlockSpec((None, d, tn), lambda j, i: (layer, 0, cb0 + j))]
    in_specs += [pl.BlockSpec(shape, imap) for _, shape, imap in extras]
    osize = jnp.dtype(out_dtype).itemsize
    nbytes = (2 * (tm * d * 2 + d * tn * 2 + tm * tn * osize) + 4 * tm * tn * 4
              + sum(2 * 4 * shape[0] * shape[1] for _, shape, _ in extras))
    if transposed_chunk is None:
        out_spec = pl.BlockSpec((tm, tn), lambda j, i: (i, j))
        out_shape = jax.ShapeDtypeStruct((r, width), out_dtype)
    else:
        tk = transposed_chunk
        assert tm % tk == 0
        out_spec = pl.BlockSpec((tm // tk, tn, tk), lambda j, i: (i, j, 0))
        out_shape = jax.ShapeDtypeStruct((r // tk, width, tk), out_dtype)
    return pl.pallas_call(
        functools.partial(_proj_kernel, epilogue=epilogue),
        grid=(width // tn, r // tm),
        in_specs=in_specs,
        out_specs=out_spec,
        out_shape=out_shape,
        compiler_params=_params(("parallel", "parallel"), nbytes),
        name="proj",
    )(n, w, *[a for a, _, _ in extras])


def _retention_kernel(lg_ref, q_ref, k_ref, v_ref, g_ref, o_ref, bw_ref, *, nc):
    c_len = RET_CHUNK
    h = pl.program_id(1)
    lgf = lg_ref[0, h]
    lgb = lg_ref[1, h]
    ii = lax.broadcasted_iota(jnp.int32, (c_len, c_len), 0).astype(F32)
    jj = lax.broadcasted_iota(jnp.int32, (c_len, c_len), 1).astype(F32)
    diff = ii - jj
    decay = jnp.where(diff >= 0, jnp.exp(jnp.maximum(diff, 0.0) * lgf),
                      jnp.exp(jnp.maximum(-diff, 0.0) * lgb))
    ci = lax.broadcasted_iota(jnp.int32, (c_len, 1), 0).astype(F32)
    xi_f = jnp.exp((ci + 1.0) * lgf)
    zeta_f = jnp.exp((c_len - 1.0 - ci) * lgf)
    xi_b = jnp.exp((c_len - ci) * lgb)
    zeta_b = jnp.exp(ci * lgb)
    one = jnp.ones((1, 1), F32)
    chunk_f = jnp.exp(one * (c_len * lgf))
    chunk_b = jnp.exp(one * (c_len * lgb))
    tn_dims = (((0,), (0,)), ((), ()))
    nt_dims = (((1,), (1,)), ((), ()))

    def rows_of(c):
        return pl.ds(pl.multiple_of(c * c_len, c_len), c_len)

    group = RET_UNROLL if nc % RET_UNROLL == 0 else 1

    def state_chain(state, kvs, chunk_decay):
        before = []
        for kv in kvs:
            before.append(state)
            state = state * chunk_decay + kv
        return before, state

    def bw_body(t, state):
        rows = [rows_of(nc - 1 - (t * group + u)) for u in range(group)]
        kvs = [lax.dot_general((k_ref[r, :].astype(F32) * zeta_b).astype(BF16), v_ref[r, :], tn_dims,
                               preferred_element_type=F32) for r in rows]
        before, state = state_chain(state, kvs, chunk_b)
        for r, s_in in zip(rows, before):
            qx = (q_ref[r, :].astype(F32) * xi_b).astype(BF16)
            bw_ref[r, :] = jnp.dot(qx, s_in.astype(BF16), preferred_element_type=F32)
        return state

    lax.fori_loop(0, nc // group, bw_body, jnp.zeros((HEAD_DIM, HEAD_DIM), F32))

    def fw_body(t, state):
        rows = [rows_of(t * group + u) for u in range(group)]
        scores = [lax.dot_general(q_ref[r, :], k_ref[r, :], nt_dims, preferred_element_type=F32)
                  for r in rows]
        kvs = [lax.dot_general((k_ref[r, :].astype(F32) * zeta_f).astype(BF16), v_ref[r, :], tn_dims,
                               preferred_element_type=F32) for r in rows]
        before, state = state_chain(state, kvs, chunk_f)
        inters = [jnp.dot((q_ref[r, :].astype(F32) * xi_f).astype(BF16), s_in.astype(BF16),
                          preferred_element_type=F32) for r, s_in in zip(rows, before)]
        for r, s, inter in zip(rows, scores, inters):
            intra = jnp.dot((s * decay).astype(BF16), v_ref[r, :], preferred_element_type=F32)
            y = (intra + inter) + bw_ref[r, :]
            y = y * lax.rsqrt(jnp.mean(y * y, axis=-1, keepdims=True) + EPS)
            o_ref[r, :] = (g_ref[r, :] * y).astype(o_ref.dtype)
        return state

    lax.fori_loop(0, nc // group, fw_body, jnp.zeros((HEAD_DIM, HEAD_DIM), F32))


def _retention(qk, v, gate, log_g, row0, batch, seq):
    assert row0 % seq == 0 and seq % RET_CHUNK == 0
    sb0 = row0 // seq
    hd = HEAD_DIM
    nbytes = 2 * seq * hd * (2 + 2 + 2 + 4 + 2) + seq * hd * 4 + 64 * hd * hd * 4
    return pl.pallas_call(
        functools.partial(_retention_kernel, nc=seq // RET_CHUNK),
        grid=(batch, RET_HEADS),
        in_specs=[pl.BlockSpec(memory_space=pltpu.SMEM),
                  pl.BlockSpec((seq, hd), lambda b, h: (sb0 + b, h)),
                  pl.BlockSpec((seq, hd), lambda b, h: (sb0 + b, RET_HEADS + h)),
                  pl.BlockSpec((seq, hd), lambda b, h: (sb0 + b, h)),
                  pl.BlockSpec((seq, hd), lambda b, h: (sb0 + b, h))],
        out_specs=pl.BlockSpec((seq, hd), lambda b, h: (b, h)),
        out_shape=jax.ShapeDtypeStruct((batch * seq, RET_HEADS * hd), BF16),
        scratch_shapes=[pltpu.VMEM((seq, hd), F32)],
        compiler_params=_params(("parallel", "parallel"), nbytes),
        name="retention",
    )(log_g, qk, qk, v, gate)


def _gqa_kernel(q_ref, k_ref, vt_ref, o_ref, qs_ref, m_ref, l_ref, acc_ref, st_ref, *, nk, tq, group, cb):
    hd = HEAD_DIM
    r = group * tq
    for g in range(group):
        qs_ref[g * tq:(g + 1) * tq, :] = q_ref[:, g * hd:(g + 1) * hd]
    m_ref[...] = jnp.full(m_ref.shape, -jnp.inf, F32)
    l_ref[...] = jnp.zeros(l_ref.shape, F32)
    acc_ref[...] = jnp.zeros(acc_ref.shape, F32)
    nt_dims = (((1,), (1,)), ((), ()))

    nb = r // cb

    def scores(c, j):
        return lax.dot_general(k_ref[c], qs_ref[j * cb:(j + 1) * cb, :], nt_dims,
                               preferred_element_type=F32)

    st_ref[...] = scores(0, 0)

    def body(c, carry):
        vt = vt_ref[c]
        st = st_ref[...]
        for j in range(nb):
            cols = slice(j * cb, (j + 1) * cb)
            if j + 1 < nb:
                st_next = scores(c, j + 1)
            else:
                st_next = scores(jnp.minimum(c + 1, nk - 1), 0)
            m_prev = m_ref[:, cols]
            m_new = jnp.maximum(m_prev, jnp.max(st, axis=0, keepdims=True))
            alpha = jnp.exp2(m_prev - m_new)
            p = jnp.exp2(st - m_new)
            l_ref[:, cols] = alpha * l_ref[:, cols] + jnp.sum(p, axis=0, keepdims=True)
            acc_ref[:, cols] = alpha * acc_ref[:, cols] + jnp.dot(vt, p.astype(BF16),
                                                                  preferred_element_type=F32)
            m_ref[:, cols] = m_new
            st = st_next
        st_ref[...] = st
        return carry

    lax.fori_loop(0, nk, body, 0)
    out = (acc_ref[...] / l_ref[...]).T
    for g in range(group):
        o_ref[:, g * hd:(g + 1) * hd] = out[g * tq:(g + 1) * tq, :].astype(o_ref.dtype)


def _gqa(aq, ak, avt, row0, batch, seq):
    hd = HEAD_DIM
    group = ATT_Q_HEADS // ATT_KV_HEADS
    tq = _tile(seq, GQA_Q_ROWS)
    tk = avt.shape[2]
    assert row0 % seq == 0 and seq % tk == 0
    sb0 = row0 // seq
    qb0 = row0 // tq
    nq = seq // tq
    nk = seq // tk
    gw = group * hd
    r = group * tq
    cb = _tile(r, GQA_COL_BLOCK)
    ak3 = ak.reshape(ak.shape[0] // tk, tk, ak.shape[1])
    nbytes = (2 * (2 * tq * gw * 2 + 2 * seq * hd * 2) + r * hd * (2 + 4) + 2 * 8 * r * 4
              + 6 * tk * cb * 4 + 2 * r * hd * 4)
    return pl.pallas_call(
        functools.partial(_gqa_kernel, nk=nk, tq=tq, group=group, cb=cb),
        grid=(batch, ATT_KV_HEADS, nq),
        in_specs=[pl.BlockSpec((tq, gw), lambda b, h, i: (qb0 + b * nq + i, h)),
                  pl.BlockSpec((nk, tk, hd), lambda b, h, i: (sb0 + b, 0, h)),
                  pl.BlockSpec((nk, hd, tk), lambda b, h, i: (sb0 + b, h, 0))],
        out_specs=pl.BlockSpec((tq, gw), lambda b, h, i: (b * nq + i, h)),
        out_shape=jax.ShapeDtypeStruct((batch * seq, ATT_Q_HEADS * hd), BF16),
        scratch_shapes=[pltpu.VMEM((r, hd), BF16),
                        pltpu.VMEM((1, r), F32),
                        pltpu.VMEM((1, r), F32),
                        pltpu.VMEM((hd, r), F32),
                        pltpu.VMEM((tk, cb), F32)],
        compiler_params=_params(("parallel", "parallel", "parallel"), nbytes),
        name="gqa",
    )(aq, ak3, avt)


def _out_proj_kernel(ret_ref, att_ref, w_ref, x_ref, o_ref, *, rw):
    acc = jnp.dot(ret_ref[...], w_ref[:rw, :], preferred_element_type=F32)
    acc = acc + jnp.dot(att_ref[...], w_ref[rw:, :], preferred_element_type=F32)
    o_ref[...] = x_ref[...] + acc


def _out_proj(ret, att, w, layer, x):
    t, d = x.shape
    rw, aw = ret.shape[1], att.shape[1]
    tm = _tile(t, ROW_TILE)
    tn = _tile(d, COL_TILE)
    nbytes = 2 * (tm * (rw + aw) * 2 + (rw + aw) * tn * 2 + 2 * tm * tn * 4) + 3 * tm * tn * 4
    return pl.pallas_call(
        functools.partial(_out_proj_kernel, rw=rw),
        grid=(d // tn, t // tm),
        in_specs=[pl.BlockSpec((tm, rw), lambda j, i: (i, 0)),
                  pl.BlockSpec((tm, aw), lambda j, i: (i, 0)),
                  pl.BlockSpec((None, rw + aw, tn), lambda j, i: (layer, 0, j)),
                  pl.BlockSpec((tm, tn), lambda j, i: (i, j))],
        out_specs=pl.BlockSpec((tm, tn), lambda j, i: (i, j)),
        out_shape=jax.ShapeDtypeStruct((t, d), F32),
        compiler_params=_params(("parallel", "parallel"), nbytes),
        name="out_proj",
    )(ret, att, w, x)


def _cross_kernel(x_ref, kv_ref, wq_ref, wo_ref, gc_ref, gf_ref, x_out_ref, n_out_ref):
    hd = MEM_HEAD_DIM
    mw = MEM_HEADS * hd
    x = x_ref[...]
    n = _rms(x, gc_ref[...]).astype(BF16)
    q = jnp.dot(n, wq_ref[...], preferred_element_type=F32).astype(BF16)
    scale = hd ** -0.5
    nt_dims = (((1,), (1,)), ((), ()))
    heads = []
    for h in range(MEM_HEADS):
        k = kv_ref[:, h * hd:(h + 1) * hd]
        v = kv_ref[:, mw + h * hd:mw + (h + 1) * hd]
        s = lax.dot_general(q[:, h * hd:(h + 1) * hd], k, nt_dims, preferred_element_type=F32) * scale
        p = jnp.exp(s - jnp.max(s, axis=-1, keepdims=True))
        p = p / jnp.sum(p, axis=-1, keepdims=True)
        heads.append(jnp.dot(p.astype(BF16), v, preferred_element_type=F32).astype(BF16))
    o = jnp.concatenate(heads, axis=-1)
    x2 = x + jnp.dot(o, wo_ref[...], preferred_element_type=F32)
    x_out_ref[...] = x2
    n_out_ref[...] = _rms(x2, gf_ref[...]).astype(n_out_ref.dtype)


def _cross(x, kv, wq, wo, layer, g_cross, g_ffn, mem_block_of_tile, tm):
    t, d = x.shape
    m = kv.shape[0] // mem_block_of_tile.n_seq
    mw = wq.shape[2]
    nbytes = (2 * (tm * d * (4 + 4 + 2) + m * 2 * mw * 2 + 2 * d * mw * 2 + 2 * d * 4)
              + 6 * tm * d * 4)
    return pl.pallas_call(
        _cross_kernel,
        grid=(t // tm,),
        in_specs=[pl.BlockSpec((tm, d), lambda i: (i, 0)),
                  pl.BlockSpec((m, 2 * mw), lambda i: (mem_block_of_tile(i), 0)),
                  pl.BlockSpec((None, d, mw), lambda i: (layer, 0, 0)),
                  pl.BlockSpec((None, mw, d), lambda i: (layer, 0, 0)),
                  pl.BlockSpec((1, d), lambda i: (0, 0)),
                  pl.BlockSpec((1, d), lambda i: (0, 0))],
        out_specs=[pl.BlockSpec((tm, d), lambda i: (i, 0)),
                   pl.BlockSpec((tm, d), lambda i: (i, 0))],
        out_shape=[jax.ShapeDtypeStruct((t, d), F32), jax.ShapeDtypeStruct((t, d), BF16)],
        compiler_params=_params(("parallel",), nbytes),
        name="cross",
    )(x, kv, wq, wo, g_cross.reshape(1, d), g_ffn.reshape(1, d))


class _SeqLayout:
    def __init__(self, n_a, len_a, n_b, len_b):
        self.n_a, self.len_a, self.n_b, self.len_b = n_a, len_a, n_b, len_b
        self.rows_a = n_a * len_a
        self.n_seq = n_a + n_b

    def seq_of_row(self, row):
        return jnp.where(row < self.rows_a, row // self.len_a,
                         self.n_a + (row - self.rows_a) // self.len_b)

    def pos_in_seq(self, row):
        return jnp.where(row < self.rows_a, row % self.len_a, (row - self.rows_a) % self.len_b)

    def seq_len_at(self, row):
        return jnp.where(row < self.rows_a, self.len_a, self.len_b)


class _MemBlockOfTile:
    def __init__(self, layout, tm):
        self.layout, self.tm, self.n_seq = layout, tm, layout.n_seq

    def __call__(self, i):
        return self.layout.seq_of_row(i * self.tm)


def _gelu(a):
    return 0.5 * a * (1.0 + lax.erf(a * (2.0 ** -0.5)))


def _ffn_kernel(n_ref, prev_ref, next_ref, wa_ref, wu_ref, cwa_ref, cwu_ref, cba_ref, cbu_ref,
                wd_ref, x_ref, o_ref, lhs_ref, h_ref, act_ref, *, layout, tm, tf, nf):
    i = pl.program_id(0)
    f = pl.program_id(1)

    def up_project():
        lhs = lhs_ref[...]
        return (jnp.dot(lhs, wa_ref[...], preferred_element_type=F32),
                jnp.dot(lhs, wu_ref[...], preferred_element_type=F32))

    def store_hidden(ha, hu):
        h_ref[:, :tf] = ha
        h_ref[:, tf:] = hu

    def conv(cols, cw_ref, cb_ref):
        lo = h_ref[pl.ds(HALO - 1, tm), cols]
        mid = h_ref[pl.ds(HALO, tm), cols]
        hi = h_ref[pl.ds(HALO + 1, tm), cols]
        return lo * cw_ref[0:1, :] + mid * cw_ref[1:2, :] + hi * cw_ref[2:3, :] + cb_ref[...]

    def gated():
        a = conv(slice(0, tf), cwa_ref, cba_ref)
        u = conv(slice(tf, 2 * tf), cwu_ref, cbu_ref)
        return (_gelu(a) * u).astype(BF16)

    @pl.when(f == 0)
    def _():
        row0 = i * tm
        pos = layout.pos_in_seq(row0)
        first = pos == 0
        last = pos + tm == layout.seq_len_at(row0)
        zero = jnp.zeros(prev_ref.shape, prev_ref.dtype)
        lhs_ref[0:HALO, :] = jnp.where(first, zero, prev_ref[...])
        lhs_ref[HALO:HALO + tm, :] = n_ref[...]
        lhs_ref[HALO + tm:, :] = jnp.where(last, zero, next_ref[...])
        o_ref[...] = x_ref[...]

    def stages(down, gate, up):
        if down:
            o_ref[...] += jnp.dot(act_ref[...], wd_ref[...], preferred_element_type=F32)
        if gate:
            act = gated()
        if up:
            ha, hu = up_project()
        if gate:
            act_ref[...] = act
        if up:
            store_hidden(ha, hu)

    pl.when(f == 0)(functools.partial(stages, False, False, True))
    pl.when(f == 1)(functools.partial(stages, False, True, nf > 1))
    if nf > 2:
        pl.when(jnp.logical_and(f >= 2, f < nf))(functools.partial(stages, True, True, True))
    if nf > 1:
        pl.when(f == nf)(functools.partial(stages, True, True, False))
    pl.when(f == nf + 1)(functools.partial(stages, True, False, False))


def _ffn(n, x, w_up, conv_w, conv_b, w_down, layer, layout):
    t, d = x.shape
    ff = w_down.shape[1]
    tm = _tile(min(layout.len_a, layout.len_b), FFN_ROWS)
    tf = _tile(ff, FFN_COLS)
    assert layout.len_a % tm == 0 and layout.len_b % tm == 0 and tm % HALO == 0
    nf = ff // tf
    hb = tm // HALO
    last_hb = t // HALO - 1
    conv_b = conv_b.reshape(1, 2 * ff)

    def up_tile(f):
        return jnp.minimum(f, nf - 1)

    def gate_tile(f):
        return jnp.clip(f - 1, 0, nf - 1)

    def down_tile(f):
        return jnp.clip(f - 2, 0, nf - 1)

    nbytes = (2 * (tm * d * 2 + 2 * HALO * d * 2 + 2 * d * tf * 2 + tf * d * 2 + tm * d * 4)
              + tm * d * 4 + (tm + 2 * HALO) * (d * 2 + 2 * tf * 4) + tm * tf * 2
              + 8 * tm * tf * 4 + tm * d * 4)
    return pl.pallas_call(
        functools.partial(_ffn_kernel, layout=layout, tm=tm, tf=tf, nf=nf),
        grid=(t // tm, nf + 2),
        in_specs=[pl.BlockSpec((tm, d), lambda i, f: (i, 0)),
                  pl.BlockSpec((HALO, d), lambda i, f: (jnp.maximum(i * hb - 1, 0), 0)),
                  pl.BlockSpec((HALO, d), lambda i, f: (jnp.minimum((i + 1) * hb, last_hb), 0)),
                  pl.BlockSpec((None, d, tf), lambda i, f: (layer, 0, up_tile(f))),
                  pl.BlockSpec((None, d, tf), lambda i, f: (layer, 0, nf + up_tile(f))),
                  pl.BlockSpec((3, tf), lambda i, f: (0, gate_tile(f))),
                  pl.BlockSpec((3, tf), lambda i, f: (0, nf + gate_tile(f))),
                  pl.BlockSpec((1, tf), lambda i, f: (0, gate_tile(f))),
                  pl.BlockSpec((1, tf), lambda i, f: (0, nf + gate_tile(f))),
                  pl.BlockSpec((None, tf, d), lambda i, f: (layer, down_tile(f), 0)),
                  pl.BlockSpec((tm, d), lambda i, f: (i, 0), pipeline_mode=pl.Buffered(1))],
        out_specs=pl.BlockSpec((tm, d), lambda i, f: (i, 0)),
        out_shape=jax.ShapeDtypeStruct((t, d), F32),
        scratch_shapes=[pltpu.VMEM((tm + 2 * HALO, d), BF16),
                        pltpu.VMEM((tm + 2 * HALO, 2 * tf), F32),
                        pltpu.VMEM((tm, tf), BF16)],
        compiler_params=_params(("parallel", "arbitrary"), nbytes),
        name="ffn",
    )(n, n, n, w_up, w_up, conv_w, conv_w, conv_b, conv_b, w_down, x)


def _rope_tables(layout):
    pos = jnp.concatenate([jnp.tile(jnp.arange(layout.len_a), layout.n_a),
                           jnp.tile(jnp.arange(layout.len_b), layout.n_b)])
    r = (pos // GRID_W).astype(F32)
    c = (pos % GRID_W).astype(F32)
    half = HEAD_DIM // 2
    freqs = ROPE_THETA ** (-jnp.arange(0, half, 2, dtype=F32) / half)
    ang_r = r[:, None] * freqs
    ang_c = c[:, None] * freqs
    cos = jnp.concatenate([jnp.cos(ang_r)] * 2 + [jnp.cos(ang_c)] * 2, axis=-1)
    sin = jnp.concatenate([-jnp.sin(ang_r), jnp.sin(ang_r), -jnp.sin(ang_c), jnp.sin(ang_c)], axis=-1)
    return cos, sin


def kernel(x_prompt, x_sample, mem_prompt, mem_sample, w_in, w_out, ret_decay_logit, q_norm, k_norm, norm_mix, norm_cross, norm_mem, w_cq, w_ckv, w_co, norm_ffn, w_up, conv_w, conv_b, w_down, norm_final):
    b, s, d = x_prompt.shape
    db, ds, _ = x_sample.shape
    m = mem_prompt.shape[1]
    depth = w_in.shape[0]
    layout = _SeqLayout(b, s, db, ds)
    t1, t2 = b * s, db * ds
    t = t1 + t2
    rw = RET_HEADS * HEAD_DIM
    aqw = ATT_Q_HEADS * HEAD_DIM
    akw = ATT_KV_HEADS * HEAD_DIM
    assert w_in.shape[2] == 4 * rw + aqw + 2 * akw and rw + aqw == w_out.shape[1]

    x = jnp.concatenate([x_prompt.reshape(t1, d), x_sample.reshape(t2, d)], axis=0)
    mem = jnp.concatenate([mem_prompt.reshape(b * m, d), mem_sample.reshape(db * m, d)], axis=0)
    cos, sin = _rope_tables(layout)
    log_g = jax.nn.log_sigmoid(ret_decay_logit.astype(F32))

    tm = _tile(t, ROW_TILE)
    tn = _tile(rw, COL_TILE)
    rope_extras = [(cos, (tm, HEAD_DIM), lambda j, i: (i, 0)),
                   (sin, (tm, HEAD_DIM), lambda j, i: (i, 0))]
    gqa_tk = _tile(min(s, ds), GQA_KV_ROWS)
    cross_tm = _tile(min(s, ds), CROSS_ROWS)
    mem_block = _MemBlockOfTile(layout, cross_tm)

    wi, wo = w_in.astype(BF16), w_out.astype(BF16)
    wcq, wckv, wco = w_cq.astype(BF16), w_ckv.astype(BF16), w_co.astype(BF16)
    wup, wdn = w_up.astype(BF16), w_down.astype(BF16)

    n = _rmsnorm(x, norm_mix[0], BF16)
    for l in range(depth):
        qk = _proj(n, wi, l, 0, 2 * rw, tn, BF16,
                   functools.partial(_epi_ret_qk, k_block0=rw // tn, k_scale=HEAD_DIM ** -0.5),
                   rope_extras)
        rv = _proj(n, wi, l, 2 * rw, rw, tn, BF16, _epi_plain)
        gate = _proj(n, wi, l, 3 * rw, rw, tn, F32, _epi_silu)
        aq = _proj(n, wi, l, 4 * rw, aqw, _tile(aqw, COL_TILE), BF16, _epi_att_q,
                   rope_extras + [(q_norm[l].reshape(1, HEAD_DIM), (1, HEAD_DIM), lambda j, i: (0, 0))])
        ak = _proj(n, wi, l, 4 * rw + aqw, akw, akw, BF16, _epi_att_k,
                   rope_extras + [(k_norm[l].reshape(1, HEAD_DIM), (1, HEAD_DIM), lambda j, i: (0, 0))])
        avt = _proj(n, wi, l, 4 * rw + aqw + akw, akw, akw, BF16, _epi_transposed_chunks,
                    transposed_chunk=gqa_tk)
        ret = jnp.concatenate([_retention(qk, rv, gate, log_g[l], 0, b, s),
                               _retention(qk, rv, gate, log_g[l], t1, db, ds)], axis=0)
        att = jnp.concatenate([_gqa(aq, ak, avt, 0, b, s), _gqa(aq, ak, avt, t1, db, ds)], axis=0)
        x = _out_proj(ret, att, wo, l, x)

        memn = _rmsnorm(mem, norm_mem[l], BF16)
        kv = _proj(memn, wckv, l, 0, w_ckv.shape[2], _tile(w_ckv.shape[2], COL_TILE),
                   BF16, _epi_plain, tm_pref=NORM_ROWS)
        x, n = _cross(x, kv, wcq, wco, l, norm_cross[l], norm_ffn[l], mem_block, cross_tm)
        x = _ffn(n, x, wup, conv_w[l], conv_b[l], wdn, l, layout)
        if l + 1 < depth:
            n = _rmsnorm(x, norm_mix[l + 1], BF16)

    y_prompt = _rmsnorm(x, norm_final, F32, 0, t1).reshape(b, s, d)
    y_sample = _rmsnorm(x, norm_final, F32, t1, t2).reshape(db, ds, d)
    return (y_prompt, y_sample)
```

```python
import functools

import jax
import jax.numpy as jnp
from jax import lax
from jax.experimental import pallas as pl
from jax.experimental.pallas import tpu as pltpu

F32 = jnp.float32
BF16 = jnp.bfloat16

HEAD_DIM = 128
RET_HEADS = 16
ATT_Q_HEADS = 16
ATT_KV_HEADS = 4
RET_CHUNK = 128
GRID_W = 64
ROPE_THETA = 10000.0
MEM_HEADS = 4
MEM_HEAD_DIM = 128
EPS = 1e-6
LOG2E = 1.4426950408889634

V7X_VMEM_BYTES = 64 * 1024 * 1024
V7X_VMEM_CAP = 60000 * 1024
LANES = 128
BF16_SUBLANES = 16

ROW_TILE = 1024
COL_TILE = 1024
NORM_ROWS = 256
CROSS_ROWS = 256
GQA_Q_ROWS = 512
GQA_KV_ROWS = 1024
GQA_COL_BLOCK = 512
RET_UNROLL = 8
FFN_ROWS = 1024
FFN_COLS = 256
HALO = BF16_SUBLANES


def _tile(n, pref):
    if n <= pref:
        return n
    t = pref
    while n % t:
        t //= 2
    return t


def _params(sem, vmem_bytes):
    limit = min(int(vmem_bytes), V7X_VMEM_CAP)
    return pltpu.CompilerParams(dimension_semantics=sem, vmem_limit_bytes=limit)


def _rms(x, g):
    y = x * lax.rsqrt(jnp.mean(x * x, axis=-1, keepdims=True) + EPS)
    return y * g


def _rmsnorm_kernel(x_ref, g_ref, o_ref):
    o_ref[...] = _rms(x_ref[...], g_ref[...]).astype(o_ref.dtype)


def _rmsnorm(x, g, out_dtype, row0=0, rows=None):
    d = x.shape[1]
    rows = x.shape[0] if rows is None else rows
    tm = _tile(rows, NORM_ROWS)
    assert row0 % tm == 0
    blk0 = row0 // tm
    nbytes = 2 * tm * d * (4 + jnp.dtype(out_dtype).itemsize) + 3 * tm * d * 4
    return pl.pallas_call(
        _rmsnorm_kernel,
        grid=(rows // tm,),
        in_specs=[pl.BlockSpec((tm, d), lambda i: (blk0 + i, 0)),
                  pl.BlockSpec((1, d), lambda i: (0, 0))],
        out_specs=pl.BlockSpec((tm, d), lambda i: (i, 0)),
        out_shape=jax.ShapeDtypeStruct((rows, d), out_dtype),
        compiler_params=_params(("parallel",), nbytes),
        name="rmsnorm",
    )(x, g.reshape(1, d))


def _rope(a, cos, sin, low_half):
    partner = jnp.where(low_half, pltpu.roll(a, 96, 1), pltpu.roll(a, 32, 1))
    return a * cos + partner * sin


def _low_half_mask(rows):
    lane = lax.broadcasted_iota(jnp.int32, (rows, HEAD_DIM), 1)
    return (lane & 32) == 0


def _proj_kernel(n_ref, w_ref, *rest, epilogue):
    acc = jnp.dot(n_ref[...], w_ref[...], preferred_element_type=F32)
    epilogue(acc, *rest)


def _epi_plain(acc, o_ref):
    o_ref[...] = acc.astype(o_ref.dtype)


def _epi_silu(acc, o_ref):
    o_ref[...] = jax.nn.silu(acc).astype(o_ref.dtype)


def _epi_ret_qk(acc, cos_ref, sin_ref, o_ref, *, k_block0, k_scale):
    j = pl.program_id(0)
    scale = jnp.where(j >= k_block0, k_scale, 1.0).astype(F32)
    cos, sin = cos_ref[...], sin_ref[...]
    low = _low_half_mask(acc.shape[0])
    for h in range(acc.shape[1] // HEAD_DIM):
        sl = slice(h * HEAD_DIM, (h + 1) * HEAD_DIM)
        o_ref[:, sl] = (_rope(acc[:, sl], cos, sin, low) * scale).astype(o_ref.dtype)


def _norm_rope_heads(acc, cos, sin, g, o_ref, scale=None):
    low = _low_half_mask(acc.shape[0])
    for h in range(acc.shape[1] // HEAD_DIM):
        sl = slice(h * HEAD_DIM, (h + 1) * HEAD_DIM)
        y = _rope(_rms(acc[:, sl], g), cos, sin, low)
        o_ref[:, sl] = (y if scale is None else y * scale).astype(o_ref.dtype)


def _epi_att_q(acc, cos_ref, sin_ref, g_ref, o_ref):
    _norm_rope_heads(acc, cos_ref[...], sin_ref[...], g_ref[...], o_ref, scale=HEAD_DIM ** -0.5 * LOG2E)


def _epi_att_k(acc, cos_ref, sin_ref, g_ref, o_ref):
    _norm_rope_heads(acc, cos_ref[...], sin_ref[...], g_ref[...], o_ref)


def _epi_transposed_chunks(acc, o_ref):
    tk = o_ref.shape[2]
    for c in range(o_ref.shape[0]):
        o_ref[c] = acc[c * tk:(c + 1) * tk, :].T.astype(o_ref.dtype)


def _proj(n, w, layer, col0, width, tn, out_dtype, epilogue, extras=(), tm_pref=None,
          transposed_chunk=None):
    r, d = n.shape
    tm = _tile(r, ROW_TILE if tm_pref is None else tm_pref)
    assert col0 % tn == 0 and width % tn == 0
    cb0 = col0 // tn
    in_specs = [pl.BlockSpec((tm, d), lambda j, i: (i, 0)),
                pl.BlockSpec((None, d, tn), lambda j, i: (layer, 0, cb0 + j))]
    in_specs += [pl.BlockSpec(shape, imap) for _, shape, imap in extras]
    osize = jnp.dtype(out_dtype).itemsize
    nbytes = (2 * (tm * d * 2 + d * tn * 2 + tm * tn * osize) + 4 * tm * tn * 4
              + sum(2 * 4 * shape[0] * shape[1] for _, shape, _ in extras))
    if transposed_chunk is None:
        out_spec = pl.BlockSpec((tm, tn), lambda j, i: (i, j))
        out_shape = jax.ShapeDtypeStruct((r, width), out_dtype)
    else:
        tk = transposed_chunk
        assert tm % tk == 0
        out_spec = pl.BlockSpec((tm // tk, tn, tk), lambda j, i: (i, j, 0))
        out_shape = jax.ShapeDtypeStruct((r // tk, width, tk), out_dtype)
    return pl.pallas_call(
        functools.partial(_proj_kernel, epilogue=epilogue),
        grid=(width // tn, r // tm),
        in_specs=in_specs,
        out_specs=out_spec,
        out_shape=out_shape,
        compiler_params=_params(("parallel", "parallel"), nbytes),
        name="proj",
    )(n, w, *[a for a, _, _ in extras])


def _retention_kernel(lg_ref, q_ref, k_ref, v_ref, g_ref, o_ref, bw_ref, *, nc):
    c_len = RET_CHUNK
    h = pl.program_id(1)
    lgf = lg_ref[0, h]
    lgb = lg_ref[1, h]
    ii = lax.broadcasted_iota(jnp.int32, (c_len, c_len), 0).astype(F32)
    jj = lax.broadcasted_iota(jnp.int32, (c_len, c_len), 1).astype(F32)
    diff = ii - jj
    decay = jnp.where(diff >= 0, jnp.exp(jnp.maximum(diff, 0.0) * lgf),
                      jnp.exp(jnp.maximum(-diff, 0.0) * lgb))
    ci = lax.broadcasted_iota(jnp.int32, (c_len, 1), 0).astype(F32)
    xi_f = jnp.exp((ci + 1.0) * lgf)
    zeta_f = jnp.exp((c_len - 1.0 - ci) * lgf)
    xi_b = jnp.exp((c_len - ci) * lgb)
    zeta_b = jnp.exp(ci * lgb)
    one = jnp.ones((1, 1), F32)
    chunk_f = jnp.exp(one * (c_len * lgf))
    chunk_b = jnp.exp(one * (c_len * lgb))
    tn_dims = (((0,), (0,)), ((), ()))
    nt_dims = (((1,), (1,)), ((), ()))

    def rows_of(c):
        return pl.ds(pl.multiple_of(c * c_len, c_len), c_len)

    group = RET_UNROLL if nc % RET_UNROLL == 0 else 1

    def state_chain(state, kvs, chunk_decay):
        before = []
        for kv in kvs:
            before.append(state)
            state = state * chunk_decay + kv
        return before, state

    def bw_body(t, state):
        rows = [rows_of(nc - 1 - (t * group + u)) for u in range(group)]
        kvs = [lax.dot_general((k_ref[r, :].astype(F32) * zeta_b).astype(BF16), v_ref[r, :], tn_dims,
                               preferred_element_type=F32) for r in rows]
        before, state = state_chain(state, kvs, chunk_b)
        for r, s_in in zip(rows, before):
            qx = (q_ref[r, :].astype(F32) * xi_b).astype(BF16)
            bw_ref[r, :] = jnp.dot(qx, s_in.astype(BF16), preferred_element_type=F32)
        return state

    lax.fori_loop(0, nc // group, bw_body, jnp.zeros((HEAD_DIM, HEAD_DIM), F32))

    def fw_body(t, state):
        rows = [rows_of(t * group + u) for u in range(group)]
        scores = [lax.dot_general(q_ref[r, :], k_ref[r, :], nt_dims, preferred_element_type=F32)
                  for r in rows]
        kvs = [lax.dot_general((k_ref[r, :].astype(F32) * zeta_f).astype(BF16), v_ref[r, :], tn_dims,
                               preferred_element_type=F32) for r in rows]
        before, state = state_chain(state, kvs, chunk_f)
        inters = [jnp.dot((q_ref[r, :].astype(F32) * xi_f).astype(BF16), s_in.astype(BF16),
                          preferred_element_type=F32) for r, s_in in zip(rows, before)]
        for r, s, inter in zip(rows, scores, inters):
            intra = jnp.dot((s * decay).astype(BF16), v_ref[r, :], preferred_element_type=F32)
            y = (intra + inter) + bw_ref[r, :]
            y = y * lax.rsqrt(jnp.mean(y * y, axis=-1, keepdims=True) + EPS)
            o_ref[r, :] = (g_ref[r, :] * y).astype(o_ref.dtype)
        return state

    lax.fori_loop(0, nc // group, fw_body, jnp.zeros((HEAD_DIM, HEAD_DIM), F32))


def _retention(qk, v, gate, log_g, row0, batch, seq):
    assert row0 % seq == 0 and seq % RET_CHUNK == 0
    sb0 = row0 // seq
    hd = HEAD_DIM
    nbytes = 2 * seq * hd * (2 + 2 + 2 + 4 + 2) + seq * hd * 4 + 64 * hd * hd * 4
    return pl.pallas_call(
        functools.partial(_retention_kernel, nc=seq // RET_CHUNK),
        grid=(batch, RET_HEADS),
        in_specs=[pl.BlockSpec(memory_space=pltpu.SMEM),
                  pl.BlockSpec((seq, hd), lambda b, h: (sb0 + b, h)),
                  pl.BlockSpec((seq, hd), lambda b, h: (sb0 + b, RET_HEADS + h)),
                  pl.BlockSpec((seq, hd), lambda b, h: (sb0 + b, h)),
                  pl.BlockSpec((seq, hd), lambda b, h: (sb0 + b, h))],
        out_specs=pl.BlockSpec((seq, hd), lambda b, h: (b, h)),
        out_shape=jax.ShapeDtypeStruct((batch * seq, RET_HEADS * hd), BF16),
        scratch_shapes=[pltpu.VMEM((seq, hd), F32)],
        compiler_params=_params(("parallel", "parallel"), nbytes),
        name="retention",
    )(log_g, qk, qk, v, gate)


def _gqa_kernel(q_ref, k_ref, vt_ref, o_ref, qs_ref, m_ref, l_ref, acc_ref, st_ref, *, nk, tq, group, cb):
    hd = HEAD_DIM
    r = group * tq
    for g in range(group):
        qs_ref[g * tq:(g + 1) * tq, :] = q_ref[:, g * hd:(g + 1) * hd]
    m_ref[...] = jnp.full(m_ref.shape, -jnp.inf, F32)
    l_ref[...] = jnp.zeros(l_ref.shape, F32)
    acc_ref[...] = jnp.zeros(acc_ref.shape, F32)
    nt_dims = (((1,), (1,)), ((), ()))

    nb = r // cb

    def scores(c, j):
        return lax.dot_general(k_ref[c], qs_ref[j * cb:(j + 1) * cb, :], nt_dims,
                               preferred_element_type=F32)

    st_ref[...] = scores(0, 0)

    def body(c, carry):
        vt = vt_ref[c]
        st = st_ref[...]
        for j in range(nb):
            cols = slice(j * cb, (j + 1) * cb)
            if j + 1 < nb:
                st_next = scores(c, j + 1)
            else:
                st_next = scores(jnp.minimum(c + 1, nk - 1), 0)
            m_prev = m_ref[:, cols]
            m_new = jnp.maximum(m_prev, jnp.max(st, axis=0, keepdims=True))
            alpha = jnp.exp2(m_prev - m_new)
            p = jnp.exp2(st - m_new)
            l_ref[:, cols] = alpha * l_ref[:, cols] + jnp.sum(p, axis=0, keepdims=True)
            acc_ref[:, cols] = alpha * acc_ref[:, cols] + jnp.dot(vt, p.astype(BF16),
                                                                  preferred_element_type=F32)
            m_ref[:, cols] = m_new
            st = st_next
        st_ref[...] = st
        return carry

    lax.fori_loop(0, nk, body, 0)
    out = (acc_ref[...] / l_ref[...]).T
    for g in range(group):
        o_ref[:, g * hd:(g + 1) * hd] = out[g * tq:(g + 1) * tq, :].astype(o_ref.dtype)


def _gqa(aq, ak, avt, row0, batch, seq):
    hd = HEAD_DIM
    group = ATT_Q_HEADS // ATT_KV_HEADS
    tq = _tile(seq, GQA_Q_ROWS)
    tk = avt.shape[2]
    assert row0 % seq == 0 and seq % tk == 0
    sb0 = row0 // seq
    qb0 = row0 // tq
    nq = seq // tq
    nk = seq // tk
    gw = group * hd
    r = group * tq
    cb = _tile(r, GQA_COL_BLOCK)
    ak3 = ak.reshape(ak.shape[0] // tk, tk, ak.shape[1])
    nbytes = (2 * (2 * tq * gw * 2 + 2 * seq * hd * 2) + r * hd * (2 + 4) + 2 * 8 * r * 4
              + 6 * tk * cb * 4 + 2 * r * hd * 4)
    return pl.pallas_call(
        functools.partial(_gqa_kernel, nk=nk, tq=tq, group=group, cb=cb),
        grid=(batch, ATT_KV_HEADS, nq),
        in_specs=[pl.BlockSpec((tq, gw), lambda b, h, i: (qb0 + b * nq + i, h)),
                  pl.BlockSpec((nk, tk, hd), lambda b, h, i: (sb0 + b, 0, h)),
                  pl.BlockSpec((nk, hd, tk), lambda b, h, i: (sb0 + b, h, 0))],
        out_specs=pl.BlockSpec((tq, gw), lambda b, h, i: (b * nq + i, h)),
        out_shape=jax.ShapeDtypeStruct((batch * seq, ATT_Q_HEADS * hd), BF16),
        scratch_shapes=[pltpu.VMEM((r, hd), BF16),
                        pltpu.VMEM((1, r), F32),
                        pltpu.VMEM((1, r), F32),
                        pltpu.VMEM((hd, r), F32),
                        pltpu.VMEM((tk, cb), F32)],
        compiler_params=_params(("parallel", "parallel", "parallel"), nbytes),
        name="gqa",
    )(aq, ak3, avt)


def _out_proj_kernel(ret_ref, att_ref, w_ref, x_ref, o_ref, *, rw):
    acc = jnp.dot(ret_ref[...], w_ref[:rw, :], preferred_element_type=F32)
    acc = acc + jnp.dot(att_ref[...], w_ref[rw:, :], preferred_element_type=F32)
    o_ref[...] = x_ref[...] + acc


def _out_proj(ret, att, w, layer, x):
    t, d = x.shape
    rw, aw = ret.shape[1], att.shape[1]
    tm = _tile(t, ROW_TILE)
    tn = _tile(d, COL_TILE)
    nbytes = 2 * (tm * (rw + aw) * 2 + (rw + aw) * tn * 2 + 2 * tm * tn * 4) + 3 * tm * tn * 4
    return pl.pallas_call(
        functools.partial(_out_proj_kernel, rw=rw),
        grid=(d // tn, t // tm),
        in_specs=[pl.BlockSpec((tm, rw), lambda j, i: (i, 0)),
                  pl.BlockSpec((tm, aw), lambda j, i: (i, 0)),
                  pl.BlockSpec((None, rw + aw, tn), lambda j, i: (layer, 0, j)),
                  pl.BlockSpec((tm, tn), lambda j, i: (i, j))],
        out_specs=pl.BlockSpec((tm, tn), lambda j, i: (i, j)),
        out_shape=jax.ShapeDtypeStruct((t, d), F32),
        compiler_params=_params(("parallel", "parallel"), nbytes),
        name="out_proj",
    )(ret, att, w, x)


def _cross_kernel(x_ref, kv_ref, wq_ref, wo_ref, gc_ref, gf_ref, x_out_ref, n_out_ref):
    hd = MEM_HEAD_DIM
    mw = MEM_HEADS * hd
    x = x_ref[...]
    n = _rms(x, gc_ref[...]).astype(BF16)
    q = jnp.dot(n, wq_ref[...], preferred_element_type=F32).astype(BF16)
    scale = hd ** -0.5
    nt_dims = (((1,), (1,)), ((), ()))
    heads = []
    for h in range(MEM_HEADS):
        k = kv_ref[:, h * hd:(h + 1) * hd]
        v = kv_ref[:, mw + h * hd:mw + (h + 1) * hd]
        s = lax.dot_general(q[:, h * hd:(h + 1) * hd], k, nt_dims, preferred_element_type=F32) * scale
        p = jnp.exp(s - jnp.max(s, axis=-1, keepdims=True))
        p = p / jnp.sum(p, axis=-1, keepdims=True)
        heads.append(jnp.dot(p.astype(BF16), v, preferred_element_type=F32).astype(BF16))
    o = jnp.concatenate(heads, axis=-1)
    x2 = x + jnp.dot(o, wo_ref[...], preferred_element_type=F32)
    x_out_ref[...] = x2
    n_out_ref[...] = _rms(x2, gf_ref[...]).astype(n_out_ref.dtype)


def _cross(x, kv, wq, wo, layer, g_cross, g_ffn, mem_block_of_tile, tm):
    t, d = x.shape
    m = kv.shape[0] // mem_block_of_tile.n_seq
    mw = wq.shape[2]
    nbytes = (2 * (tm * d * (4 + 4 + 2) + m * 2 * mw * 2 + 2 * d * mw * 2 + 2 * d * 4)
              + 6 * tm * d * 4)
    return pl.pallas_call(
        _cross_kernel,
        grid=(t // tm,),
        in_specs=[pl.BlockSpec((tm, d), lambda i: (i, 0)),
                  pl.BlockSpec((m, 2 * mw), lambda i: (mem_block_of_tile(i), 0)),
                  pl.BlockSpec((None, d, mw), lambda i: (layer, 0, 0)),
                  pl.BlockSpec((None, mw, d), lambda i: (layer, 0, 0)),
                  pl.BlockSpec((1, d), lambda i: (0, 0)),
                  pl.BlockSpec((1, d), lambda i: (0, 0))],
        out_specs=[pl.BlockSpec((tm, d), lambda i: (i, 0)),
                   pl.BlockSpec((tm, d), lambda i: (i, 0))],
        out_shape=[jax.ShapeDtypeStruct((t, d), F32), jax.ShapeDtypeStruct((t, d), BF16)],
        compiler_params=_params(("parallel",), nbytes),
        name="cross",
    )(x, kv, wq, wo, g_cross.reshape(1, d), g_ffn.reshape(1, d))


class _SeqLayout:
    def __init__(self, n_a, len_a, n_b, len_b):
        self.n_a, self.len_a, self.n_b, self.len_b = n_a, len_a, n_b, len_b
        self.rows_a = n_a * len_a
        self.n_seq = n_a + n_b

    def seq_of_row(self, row):
        return jnp.where(row < self.rows_a, row // self.len_a,
                         self.n_a + (row - self.rows_a) // self.len_b)

    def pos_in_seq(self, row):
        return jnp.where(row < self.rows_a, row % self.len_a, (row - self.rows_a) % self.len_b)

    def seq_len_at(self, row):
        return jnp.where(row < self.rows_a, self.len_a, self.len_b)


class _MemBlockOfTile:
    def __init__(self, layout, tm):
        self.layout, self.tm, self.n_seq = layout, tm, layout.n_seq

    def __call__(self, i):
        return self.layout.seq_of_row(i * self.tm)


def _gelu(a):
    return 0.5 * a * (1.0 + lax.erf(a * (2.0 ** -0.5)))


def _ffn_kernel(n_hbm, wa_ref, wu_ref, cwa_ref, cwu_ref, cba_ref, cbu_ref, wd_ref, x_hbm, o_hbm,
                lhs_ref, h_ref, acc_ref, sems, *, layout, tm, tf, nf):
    i = pl.program_id(0)
    f = pl.program_id(1)
    row0 = pl.multiple_of(i * tm, tm)
    first_down = 1

    def residual_copy():
        return pltpu.make_async_copy(x_hbm.at[pl.ds(row0, tm), :], acc_ref, sems.at[0])

    def rows_copy(src_row, dst_row, rows, sem):
        return pltpu.make_async_copy(n_hbm.at[pl.ds(src_row, rows), :],
                                     lhs_ref.at[pl.ds(dst_row, rows), :], sems.at[sem])

    slabs = tf // LANES

    def conv(slab, cw, cb):
        lo = h_ref[slab, pl.ds(HALO - 1, tm), :]
        mid = h_ref[slab, pl.ds(HALO, tm), :]
        hi = h_ref[slab, pl.ds(HALO + 1, tm), :]
        return lo * cw[0:1, :] + mid * cw[1:2, :] + hi * cw[2:3, :] + cb

    def gated():
        out = []
        for s in range(slabs):
            lanes = slice(s * LANES, (s + 1) * LANES)
            a = conv(s, cwa_ref[:, lanes], cba_ref[:, lanes])
            u = conv(slabs + s, cwu_ref[:, lanes], cbu_ref[:, lanes])
            out.append((_gelu(a) * u).astype(BF16))
        return jnp.concatenate(out, axis=-1)

    def store_hidden(ha, hu):
        for s in range(slabs):
            lanes = slice(s * LANES, (s + 1) * LANES)
            h_ref[s] = ha[:, lanes]
            h_ref[slabs + s] = hu[:, lanes]

    @pl.when(f == 0)
    def _():
        residual_copy().start()
        pos = layout.pos_in_seq(row0)
        first = pos == 0
        last = pos + tm == layout.seq_len_at(row0)
        main = rows_copy(row0, HALO, tm, 1)
        main.start()
        zero = jnp.zeros((HALO, lhs_ref.shape[1]), lhs_ref.dtype)

        @pl.when(first)
        def _():
            lhs_ref[0:HALO, :] = zero

        @pl.when(jnp.logical_not(first))
        def _():
            prev = rows_copy(row0 - HALO, 0, HALO, 2)
            prev.start()
            prev.wait()

        @pl.when(last)
        def _():
            lhs_ref[HALO + tm:, :] = zero

        @pl.when(jnp.logical_not(last))
        def _():
            nxt = rows_copy(row0 + tm, HALO + tm, HALO, 3)
            nxt.start()
            nxt.wait()

        main.wait()

    @pl.when(f == first_down)
    def _():
        residual_copy().wait()

    def stages(gate_down, up):
        if gate_down:
            acc_ref[...] += jnp.dot(gated(), wd_ref[...], preferred_element_type=F32)
        if up:
            lhs = lhs_ref[...]
            ha = jnp.dot(lhs, wa_ref[...], preferred_element_type=F32)
            hu = jnp.dot(lhs, wu_ref[...], preferred_element_type=F32)
            store_hidden(ha, hu)

    pl.when(f == 0)(functools.partial(stages, False, True))
    if nf > 1:
        pl.when(jnp.logical_and(f >= 1, f < nf))(functools.partial(stages, True, True))
    pl.when(f == nf)(functools.partial(stages, True, False))

    @pl.when(f == nf)
    def _():
        out = pltpu.make_async_copy(acc_ref, o_hbm.at[pl.ds(row0, tm), :], sems.at[4])
        out.start()
        out.wait()


def _ffn(n, x, w_up, conv_w, conv_b, w_down, layer, layout):
    t, d = x.shape
    ff = w_down.shape[1]
    tm = _tile(min(layout.len_a, layout.len_b), FFN_ROWS)
    tf = _tile(ff, FFN_COLS)
    assert layout.len_a % tm == 0 and layout.len_b % tm == 0 and tm % HALO == 0
    nf = ff // tf
    conv_b = conv_b.reshape(1, 2 * ff)

    def up_tile(f):
        return jnp.minimum(f, nf - 1)

    def gate_tile(f):
        return jnp.maximum(f - 1, 0)

    down_tile = gate_tile
    hidden_rows = tm + 2 * HALO
    nbytes = (2 * (2 * d * tf * 2 + tf * d * 2) + hidden_rows * (d * 2 + 2 * tf * 4)
              + tm * d * 4 + 4 * hidden_rows * tf * 4 + 3 * tm * tf * 4)
    return pl.pallas_call(
        functools.partial(_ffn_kernel, layout=layout, tm=tm, tf=tf, nf=nf),
        grid=(t // tm, nf + 1),
        in_specs=[pl.BlockSpec(memory_space=pl.ANY),
                  pl.BlockSpec((None, d, tf), lambda i, f: (layer, 0, up_tile(f))),
                  pl.BlockSpec((None, d, tf), lambda i, f: (layer, 0, nf + up_tile(f))),
                  pl.BlockSpec((3, tf), lambda i, f: (0, gate_tile(f))),
                  pl.BlockSpec((3, tf), lambda i, f: (0, nf + gate_tile(f))),
                  pl.BlockSpec((1, tf), lambda i, f: (0, gate_tile(f))),
                  pl.BlockSpec((1, tf), lambda i, f: (0, nf + gate_tile(f))),
                  pl.BlockSpec((None, tf, d), lambda i, f: (layer, down_tile(f), 0)),
                  pl.BlockSpec(memory_space=pl.ANY)],
        out_specs=pl.BlockSpec(memory_space=pl.ANY),
        out_shape=jax.ShapeDtypeStruct((t, d), F32),
        scratch_shapes=[pltpu.VMEM((hidden_rows, d), BF16),
                        pltpu.VMEM((2 * tf // LANES, hidden_rows, LANES), F32),
                        pltpu.VMEM((tm, d), F32),
                        pltpu.SemaphoreType.DMA((5,))],
        compiler_params=_params(("arbitrary", "arbitrary"), nbytes),
        name="ffn",
    )(n, w_up, w_up, conv_w, conv_w, conv_b, conv_b, w_down, x)


def _rope_tables(layout):
    pos = jnp.concatenate([jnp.tile(jnp.arange(layout.len_a), layout.n_a),
                           jnp.tile(jnp.arange(layout.len_b), layout.n_b)])
    r = (pos // GRID_W).astype(F32)
    c = (pos % GRID_W).astype(F32)
    half = HEAD_DIM // 2
    freqs = ROPE_THETA ** (-jnp.arange(0, half, 2, dtype=F32) / half)
    ang_r = r[:, None] * freqs
    ang_c = c[:, None] * freqs
    cos = jnp.concatenate([jnp.cos(ang_r)] * 2 + [jnp.cos(ang_c)] * 2, axis=-1)
    sin = jnp.concatenate([-jnp.sin(ang_r), jnp.sin(ang_r), -jnp.sin(ang_c), jnp.sin(ang_c)], axis=-1)
    return cos, sin


def kernel(x_prompt, x_sample, mem_prompt, mem_sample, w_in, w_out, ret_decay_logit, q_norm, k_norm, norm_mix, norm_cross, norm_mem, w_cq, w_ckv, w_co, norm_ffn, w_up, conv_w, conv_b, w_down, norm_final):
    b, s, d = x_prompt.shape
    db, ds, _ = x_sample.shape
    m = mem_prompt.shape[1]
    depth = w_in.shape[0]
    layout = _SeqLayout(b, s, db, ds)
    t1, t2 = b * s, db * ds
    t = t1 + t2
    rw = RET_HEADS * HEAD_DIM
    aqw = ATT_Q_HEADS * HEAD_DIM
    akw = ATT_KV_HEADS * HEAD_DIM
    assert w_in.shape[2] == 4 * rw + aqw + 2 * akw and rw + aqw == w_out.shape[1]

    x = jnp.concatenate([x_prompt.reshape(t1, d), x_sample.reshape(t2, d)], axis=0)
    mem = jnp.concatenate([mem_prompt.reshape(b * m, d), mem_sample.reshape(db * m, d)], axis=0)
    cos, sin = _rope_tables(layout)
    log_g = jax.nn.log_sigmoid(ret_decay_logit.astype(F32))

    tm = _tile(t, ROW_TILE)
    tn = _tile(rw, COL_TILE)
    rope_extras = [(cos, (tm, HEAD_DIM), lambda j, i: (i, 0)),
                   (sin, (tm, HEAD_DIM), lambda j, i: (i, 0))]
    gqa_tk = _tile(min(s, ds), GQA_KV_ROWS)
    cross_tm = _tile(min(s, ds), CROSS_ROWS)
    mem_block = _MemBlockOfTile(layout, cross_tm)

    wi, wo = w_in.astype(BF16), w_out.astype(BF16)
    wcq, wckv, wco = w_cq.astype(BF16), w_ckv.astype(BF16), w_co.astype(BF16)
    wup, wdn = w_up.astype(BF16), w_down.astype(BF16)

    n = _rmsnorm(x, norm_mix[0], BF16)
    for l in range(depth):
        qk = _proj(n, wi, l, 0, 2 * rw, tn, BF16,
                   functools.partial(_epi_ret_qk, k_block0=rw // tn, k_scale=HEAD_DIM ** -0.5),
                   rope_extras)
        rv = _proj(n, wi, l, 2 * rw, rw, tn, BF16, _epi_plain)
        gate = _proj(n, wi, l, 3 * rw, rw, tn, F32, _epi_silu)
        aq = _proj(n, wi, l, 4 * rw, aqw, _tile(aqw, COL_TILE), BF16, _epi_att_q,
                   rope_extras + [(q_norm[l].reshape(1, HEAD_DIM), (1, HEAD_DIM), lambda j, i: (0, 0))])
        ak = _proj(n, wi, l, 4 * rw + aqw, akw, akw, BF16, _epi_att_k,
                   rope_extras + [(k_norm[l].reshape(1, HEAD_DIM), (1, HEAD_DIM), lambda j, i: (0, 0))])
        avt = _proj(n, wi, l, 4 * rw + aqw + akw, akw, akw, BF16, _epi_transposed_chunks,
                    transposed_chunk=gqa_tk)
        ret = jnp.concatenate([_retention(qk, rv, gate, log_g[l], 0, b, s),
                               _retention(qk, rv, gate, log_g[l], t1, db, ds)], axis=0)
        att = jnp.concatenate([_gqa(aq, ak, avt, 0, b, s), _gqa(aq, ak, avt, t1, db, ds)], axis=0)
        x = _out_proj(ret, att, wo, l, x)

        memn = _rmsnorm(mem, norm_mem[l], BF16)
        kv = _proj(memn, wckv, l, 0, w_ckv.shape[2], _tile(w_ckv.shape[2], COL_TILE),
                   BF16, _epi_plain, tm_pref=NORM_ROWS)
        x, n = _cross(x, kv, wcq, wco, l, norm_cross[l], norm_ffn[l], mem_block, cross_tm)
        x = _ffn(n, x, wup, conv_w[l], conv_b[l], wdn, l, layout)
        if l + 1 < depth:
            n = _rmsnorm(x, norm_mix[l + 1], BF16)

    y_prompt = _rmsnorm(x, norm_final, F32, 0, t1).reshape(b, s, d)
    y_sample = _rmsnorm(x, norm_final, F32, t1, t2).reshape(db, ds, d)
    return (y_prompt, y_sample)
```

```python
import functools

import jax
import jax.numpy as jnp
from jax import lax
from jax.experimental import pallas as pl
from jax.experimental.pallas import tpu as pltpu

F32 = jnp.float32
BF16 = jnp.bfloat16

HEAD_DIM = 128
RET_HEADS = 16
ATT_Q_HEADS = 16
ATT_KV_HEADS = 4
RET_CHUNK = 128
GRID_W = 64
ROPE_THETA = 10000.0
MEM_HEADS = 4
MEM_HEAD_DIM = 128
EPS = 1e-6
LOG2E = 1.4426950408889634

V7X_VMEM_BYTES = 64 * 1024 * 1024
V7X_VMEM_CAP = 60000 * 1024
LANES = 128
BF16_SUBLANES = 16

ROW_TILE = 1024
COL_TILE = 1024
NORM_ROWS = 256
CROSS_ROWS = 256
GQA_Q_ROWS = 512
GQA_KV_ROWS = 1024
GQA_COL_BLOCK = 512
RET_UNROLL = 8
FFN_ROWS = 1024
FFN_COLS = 256
HALO = BF16_SUBLANES
NORM_CHUNK = 64


def _tile(n, pref):
    if n <= pref:
        return n
    t = pref
    while n % t:
        t //= 2
    return t


def _params(sem, vmem_bytes):
    limit = min(int(vmem_bytes), V7X_VMEM_CAP)
    return pltpu.CompilerParams(dimension_semantics=sem, vmem_limit_bytes=limit)


def _rms(x, g):
    y = x * lax.rsqrt(jnp.mean(x * x, axis=-1, keepdims=True) + EPS)
    return y * g


def _rmsnorm_kernel(x_ref, g_ref, o_ref):
    o_ref[...] = _rms(x_ref[...], g_ref[...]).astype(o_ref.dtype)


def _rmsnorm(x, g, out_dtype, row0=0, rows=None):
    d = x.shape[1]
    rows = x.shape[0] if rows is None else rows
    tm = _tile(rows, NORM_ROWS)
    assert row0 % tm == 0
    blk0 = row0 // tm
    nbytes = 2 * tm * d * (4 + jnp.dtype(out_dtype).itemsize) + 3 * tm * d * 4
    return pl.pallas_call(
        _rmsnorm_kernel,
        grid=(rows // tm,),
        in_specs=[pl.BlockSpec((tm, d), lambda i: (blk0 + i, 0)),
                  pl.BlockSpec((1, d), lambda i: (0, 0))],
        out_specs=pl.BlockSpec((tm, d), lambda i: (i, 0)),
        out_shape=jax.ShapeDtypeStruct((rows, d), out_dtype),
        compiler_params=_params(("parallel",), nbytes),
        name="rmsnorm",
    )(x, g.reshape(1, d))


def _rope(a, cos, sin, low_half):
    partner = jnp.where(low_half, pltpu.roll(a, 96, 1), pltpu.roll(a, 32, 1))
    return a * cos + partner * sin


def _low_half_mask(rows):
    lane = lax.broadcasted_iota(jnp.int32, (rows, HEAD_DIM), 1)
    return (lane & 32) == 0


def _proj_kernel(n_ref, w_ref, *rest, epilogue):
    acc = jnp.dot(n_ref[...], w_ref[...], preferred_element_type=F32)
    epilogue(acc, *rest)


def _epi_plain(acc, o_ref):
    o_ref[...] = acc.astype(o_ref.dtype)


def _epi_silu(acc, o_ref):
    o_ref[...] = jax.nn.silu(acc).astype(o_ref.dtype)


def _epi_ret_qk(acc, cos_ref, sin_ref, o_ref, *, k_block0, k_scale):
    j = pl.program_id(0)
    scale = jnp.where(j >= k_block0, k_scale, 1.0).astype(F32)
    cos, sin = cos_ref[...], sin_ref[...]
    low = _low_half_mask(acc.shape[0])
    for h in range(acc.shape[1] // HEAD_DIM):
        sl = slice(h * HEAD_DIM, (h + 1) * HEAD_DIM)
        o_ref[:, sl] = (_rope(acc[:, sl], cos, sin, low) * scale).astype(o_ref.dtype)


def _norm_rope_heads(acc, cos, sin, g, o_ref, scale=None):
    low = _low_half_mask(acc.shape[0])
    for h in range(acc.shape[1] // HEAD_DIM):
        sl = slice(h * HEAD_DIM, (h + 1) * HEAD_DIM)
        y = _rope(_rms(acc[:, sl], g), cos, sin, low)
        o_ref[:, sl] = (y if scale is None else y * scale).astype(o_ref.dtype)


def _epi_att_q(acc, cos_ref, sin_ref, g_ref, o_ref):
    _norm_rope_heads(acc, cos_ref[...], sin_ref[...], g_ref[...], o_ref, scale=HEAD_DIM ** -0.5 * LOG2E)


def _epi_att_k(acc, cos_ref, sin_ref, g_ref, o_ref):
    _norm_rope_heads(acc, cos_ref[...], sin_ref[...], g_ref[...], o_ref)


def _epi_transposed_chunks(acc, o_ref):
    tk = o_ref.shape[2]
    for c in range(o_ref.shape[0]):
        o_ref[c] = acc[c * tk:(c + 1) * tk, :].T.astype(o_ref.dtype)


def _proj(n, w, layer, col0, width, tn, out_dtype, epilogue, extras=(), tm_pref=None,
          transposed_chunk=None):
    r, d = n.shape
    tm = _tile(r, ROW_TILE if tm_pref is None else tm_pref)
    assert col0 % tn == 0 and width % tn == 0
    cb0 = col0 // tn
    in_specs = [pl.BlockSpec((tm, d), lambda j, i: (i, 0)),
                pl.BlockSpec((None, d, tn), lambda j, i: (layer, 0, cb0 + j))]
    in_specs += [pl.BlockSpec(shape, imap) for _, shape, imap in extras]
    osize = jnp.dtype(out_dtype).itemsize
    nbytes = (2 * (tm * d * 2 + d * tn * 2 + tm * tn * osize) + 4 * tm * tn * 4
              + sum(2 * 4 * shape[0] * shape[1] for _, shape, _ in extras))
    if transposed_chunk is None:
        out_spec = pl.BlockSpec((tm, tn), lambda j, i: (i, j))
        out_shape = jax.ShapeDtypeStruct((r, width), out_dtype)
    else:
        tk = transposed_chunk
        assert tm % tk == 0
        out_spec = pl.BlockSpec((tm // tk, tn, tk), lambda j, i: (i, j, 0))
        out_shape = jax.ShapeDtypeStruct((r // tk, width, tk), out_dtype)
    return pl.pallas_call(
        functools.partial(_proj_kernel, epilogue=epilogue),
        grid=(width // tn, r // tm),
        in_specs=in_specs,
        out_specs=out_spec,
        out_shape=out_shape,
        compiler_params=_params(("parallel", "parallel"), nbytes),
        name="proj",
    )(n, w, *[a for a, _, _ in extras])


def _proj_deferred_kernel(n_ref, w_ref, *rest, epilogue, ni):
    *epi_refs, prev_ref = rest
    i = pl.program_id(1)

    def step(multiply, finish):
        if multiply:
            acc = jnp.dot(n_ref[...], w_ref[...], preferred_element_type=F32)
        if finish:
            epilogue(prev_ref, *epi_refs)
        if multiply:
            prev_ref[...] = acc

    pl.when(i == 0)(functools.partial(step, True, False))
    if ni > 1:
        pl.when(jnp.logical_and(i > 0, i < ni))(functools.partial(step, True, True))
    pl.when(i == ni)(functools.partial(step, False, True))


def _proj_deferred(n, w, layer, col0, width, tn, out_dtype, epilogue, extras=()):
    r, d = n.shape
    tm = _tile(r, ROW_TILE)
    assert col0 % tn == 0 and width % tn == 0
    cb0 = col0 // tn
    ni = r // tm

    def mul_tile(i):
        return jnp.minimum(i, ni - 1)

    def epi_tile(i):
        return jnp.maximum(i - 1, 0)

    in_specs = [pl.BlockSpec((tm, d), lambda j, i: (mul_tile(i), 0)),
                pl.BlockSpec((None, d, tn), lambda j, i: (layer, 0, cb0 + j))]
    in_specs += [pl.BlockSpec(shape, functools.partial(lambda imap, j, i: imap(j, epi_tile(i)), imap))
                 for _, shape, imap in extras]
    osize = jnp.dtype(out_dtype).itemsize
    nbytes = (2 * (tm * d * 2 + d * tn * 2 + tm * tn * osize) + 5 * tm * tn * 4
              + sum(2 * 4 * shape[0] * shape[1] for _, shape, _ in extras))
    return pl.pallas_call(
        functools.partial(_proj_deferred_kernel, epilogue=epilogue, ni=ni),
        grid=(width // tn, ni + 1),
        in_specs=in_specs,
        out_specs=pl.BlockSpec((tm, tn), lambda j, i: (epi_tile(i), j)),
        out_shape=jax.ShapeDtypeStruct((r, width), out_dtype),
        scratch_shapes=[pltpu.VMEM((tm, tn), F32)],
        compiler_params=_params(("parallel", "arbitrary"), nbytes),
        name="proj_deferred",
    )(n, w, *[a for a, _, _ in extras])


def _retention_kernel(lg_ref, q_ref, k_ref, v_ref, g_ref, o_ref, bw_ref, *, nc):
    c_len = RET_CHUNK
    h = pl.program_id(1)
    lgf = lg_ref[0, h]
    lgb = lg_ref[1, h]
    ii = lax.broadcasted_iota(jnp.int32, (c_len, c_len), 0).astype(F32)
    jj = lax.broadcasted_iota(jnp.int32, (c_len, c_len), 1).astype(F32)
    diff = ii - jj
    decay = jnp.where(diff >= 0, jnp.exp(jnp.maximum(diff, 0.0) * lgf),
                      jnp.exp(jnp.maximum(-diff, 0.0) * lgb))
    ci = lax.broadcasted_iota(jnp.int32, (c_len, 1), 0).astype(F32)
    xi_f = jnp.exp((ci + 1.0) * lgf)
    zeta_f = jnp.exp((c_len - 1.0 - ci) * lgf)
    xi_b = jnp.exp((c_len - ci) * lgb)
    zeta_b = jnp.exp(ci * lgb)
    one = jnp.ones((1, 1), F32)
    chunk_f = jnp.exp(one * (c_len * lgf))
    chunk_b = jnp.exp(one * (c_len * lgb))
    tn_dims = (((0,), (0,)), ((), ()))
    nt_dims = (((1,), (1,)), ((), ()))

    def rows_of(c):
        return pl.ds(pl.multiple_of(c * c_len, c_len), c_len)

    group = RET_UNROLL if nc % RET_UNROLL == 0 else 1

    def state_chain(state, kvs, chunk_decay):
        before = []
        for kv in kvs:
            before.append(state)
            state = state * chunk_decay + kv
        return before, state

    def bw_body(t, state):
        rows = [rows_of(nc - 1 - (t * group + u)) for u in range(group)]
        kvs = [lax.dot_general((k_ref[r, :].astype(F32) * zeta_b).astype(BF16), v_ref[r, :], tn_dims,
                               preferred_element_type=F32) for r in rows]
        before, state = state_chain(state, kvs, chunk_b)
        for r, s_in in zip(rows, before):
            qx = (q_ref[r, :].astype(F32) * xi_b).astype(BF16)
            bw_ref[r, :] = jnp.dot(qx, s_in.astype(BF16), preferred_element_type=F32)
        return state

    lax.fori_loop(0, nc // group, bw_body, jnp.zeros((HEAD_DIM, HEAD_DIM), F32))

    def fw_body(t, state):
        rows = [rows_of(t * group + u) for u in range(group)]
        scores = [lax.dot_general(q_ref[r, :], k_ref[r, :], nt_dims, preferred_element_type=F32)
                  for r in rows]
        kvs = [lax.dot_general((k_ref[r, :].astype(F32) * zeta_f).astype(BF16), v_ref[r, :], tn_dims,
                               preferred_element_type=F32) for r in rows]
        before, state = state_chain(state, kvs, chunk_f)
        inters = [jnp.dot((q_ref[r, :].astype(F32) * xi_f).astype(BF16), s_in.astype(BF16),
                          preferred_element_type=F32) for r, s_in in zip(rows, before)]
        for r, s, inter in zip(rows, scores, inters):
            intra = jnp.dot((s * decay).astype(BF16), v_ref[r, :], preferred_element_type=F32)
            y = (intra + inter) + bw_ref[r, :]
            y = y * lax.rsqrt(jnp.mean(y * y, axis=-1, keepdims=True) + EPS)
            o_ref[r, :] = (g_ref[r, :] * y).astype(o_ref.dtype)
        return state

    lax.fori_loop(0, nc // group, fw_body, jnp.zeros((HEAD_DIM, HEAD_DIM), F32))


def _retention(qk, v, gate, log_g, row0, batch, seq):
    assert row0 % seq == 0 and seq % RET_CHUNK == 0
    sb0 = row0 // seq
    hd = HEAD_DIM
    nbytes = 2 * seq * hd * (2 + 2 + 2 + 4 + 2) + seq * hd * 4 + 64 * hd * hd * 4
    return pl.pallas_call(
        functools.partial(_retention_kernel, nc=seq // RET_CHUNK),
        grid=(batch, RET_HEADS),
        in_specs=[pl.BlockSpec(memory_space=pltpu.SMEM),
                  pl.BlockSpec((seq, hd), lambda b, h: (sb0 + b, h)),
                  pl.BlockSpec((seq, hd), lambda b, h: (sb0 + b, RET_HEADS + h)),
                  pl.BlockSpec((seq, hd), lambda b, h: (sb0 + b, h)),
                  pl.BlockSpec((seq, hd), lambda b, h: (sb0 + b, h))],
        out_specs=pl.BlockSpec((seq, hd), lambda b, h: (b, h)),
        out_shape=jax.ShapeDtypeStruct((batch * seq, RET_HEADS * hd), BF16),
        scratch_shapes=[pltpu.VMEM((seq, hd), F32)],
        compiler_params=_params(("parallel", "parallel"), nbytes),
        name="retention",
    )(log_g, qk, qk, v, gate)


def _gqa_kernel(q_ref, k_ref, vt_ref, o_ref, qs_ref, m_ref, l_ref, acc_ref, st_ref, *, nk, tq, group, cb):
    hd = HEAD_DIM
    r = group * tq
    for g in range(group):
        qs_ref[g * tq:(g + 1) * tq, :] = q_ref[:, g * hd:(g + 1) * hd]
    m_ref[...] = jnp.full(m_ref.shape, -jnp.inf, F32)
    l_ref[...] = jnp.zeros(l_ref.shape, F32)
    acc_ref[...] = jnp.zeros(acc_ref.shape, F32)
    nt_dims = (((1,), (1,)), ((), ()))

    nb = r // cb

    def scores(c, j):
        return lax.dot_general(k_ref[c], qs_ref[j * cb:(j + 1) * cb, :], nt_dims,
                               preferred_element_type=F32)

    st_ref[...] = scores(0, 0)

    def body(c, carry):
        vt = vt_ref[c]
        st = st_ref[...]
        for j in range(nb):
            cols = slice(j * cb, (j + 1) * cb)
            if j + 1 < nb:
                st_next = scores(c, j + 1)
            else:
                st_next = scores(jnp.minimum(c + 1, nk - 1), 0)
            m_prev = m_ref[:, cols]
            m_new = jnp.maximum(m_prev, jnp.max(st, axis=0, keepdims=True))
            alpha = jnp.exp2(m_prev - m_new)
            p = jnp.exp2(st - m_new)
            l_ref[:, cols] = alpha * l_ref[:, cols] + jnp.sum(p, axis=0, keepdims=True)
            acc_ref[:, cols] = alpha * acc_ref[:, cols] + jnp.dot(vt, p.astype(BF16),
                                                                  preferred_element_type=F32)
            m_ref[:, cols] = m_new
            st = st_next
        st_ref[...] = st
        return carry

    lax.fori_loop(0, nk, body, 0)
    out = (acc_ref[...] / l_ref[...]).T
    for g in range(group):
        o_ref[:, g * hd:(g + 1) * hd] = out[g * tq:(g + 1) * tq, :].astype(o_ref.dtype)


def _gqa(aq, ak, avt, row0, batch, seq):
    hd = HEAD_DIM
    group = ATT_Q_HEADS // ATT_KV_HEADS
    tq = _tile(seq, GQA_Q_ROWS)
    tk = avt.shape[2]
    assert row0 % seq == 0 and seq % tk == 0
    sb0 = row0 // seq
    qb0 = row0 // tq
    nq = seq // tq
    nk = seq // tk
    gw = group * hd
    r = group * tq
    cb = _tile(r, GQA_COL_BLOCK)
    ak3 = ak.reshape(ak.shape[0] // tk, tk, ak.shape[1])
    nbytes = (2 * (2 * tq * gw * 2 + 2 * seq * hd * 2) + r * hd * (2 + 4) + 2 * 8 * r * 4
              + 6 * tk * cb * 4 + 2 * r * hd * 4)
    return pl.pallas_call(
        functools.partial(_gqa_kernel, nk=nk, tq=tq, group=group, cb=cb),
        grid=(batch, ATT_KV_HEADS, nq),
        in_specs=[pl.BlockSpec((tq, gw), lambda b, h, i: (qb0 + b * nq + i, h)),
                  pl.BlockSpec((nk, tk, hd), lambda b, h, i: (sb0 + b, 0, h)),
                  pl.BlockSpec((nk, hd, tk), lambda b, h, i: (sb0 + b, h, 0))],
        out_specs=pl.BlockSpec((tq, gw), lambda b, h, i: (b * nq + i, h)),
        out_shape=jax.ShapeDtypeStruct((batch * seq, ATT_Q_HEADS * hd), BF16),
        scratch_shapes=[pltpu.VMEM((r, hd), BF16),
                        pltpu.VMEM((1, r), F32),
                        pltpu.VMEM((1, r), F32),
                        pltpu.VMEM((hd, r), F32),
                        pltpu.VMEM((tk, cb), F32)],
        compiler_params=_params(("parallel", "parallel", "parallel"), nbytes),
        name="gqa",
    )(aq, ak3, avt)


def _out_proj_kernel(ret_ref, att_ref, w_ref, x_ref, o_ref, *, rw):
    acc = jnp.dot(ret_ref[...], w_ref[:rw, :], preferred_element_type=F32)
    acc = acc + jnp.dot(att_ref[...], w_ref[rw:, :], preferred_element_type=F32)
    o_ref[...] = x_ref[...] + acc


def _out_proj(ret, att, w, layer, x):
    t, d = x.shape
    rw, aw = ret.shape[1], att.shape[1]
    tm = _tile(t, ROW_TILE)
    tn = _tile(d, COL_TILE)
    nbytes = 2 * (tm * (rw + aw) * 2 + (rw + aw) * tn * 2 + 2 * tm * tn * 4) + 3 * tm * tn * 4
    return pl.pallas_call(
        functools.partial(_out_proj_kernel, rw=rw),
        grid=(d // tn, t // tm),
        in_specs=[pl.BlockSpec((tm, rw), lambda j, i: (i, 0)),
                  pl.BlockSpec((tm, aw), lambda j, i: (i, 0)),
                  pl.BlockSpec((None, rw + aw, tn), lambda j, i: (layer, 0, j)),
                  pl.BlockSpec((tm, tn), lambda j, i: (i, j))],
        out_specs=pl.BlockSpec((tm, tn), lambda j, i: (i, j)),
        out_shape=jax.ShapeDtypeStruct((t, d), F32),
        compiler_params=_params(("parallel", "parallel"), nbytes),
        name="out_proj",
    )(ret, att, w, x)


def _cross_kernel(x_ref, kv_ref, wq_ref, wo_ref, gc_ref, gf_ref, x_out_ref, n_out_ref):
    hd = MEM_HEAD_DIM
    mw = MEM_HEADS * hd
    x = x_ref[...]
    n = _rms(x, gc_ref[...]).astype(BF16)
    q = jnp.dot(n, wq_ref[...], preferred_element_type=F32).astype(BF16)
    scale = hd ** -0.5
    nt_dims = (((1,), (1,)), ((), ()))
    heads = []
    for h in range(MEM_HEADS):
        k = kv_ref[:, h * hd:(h + 1) * hd]
        v = kv_ref[:, mw + h * hd:mw + (h + 1) * hd]
        s = lax.dot_general(q[:, h * hd:(h + 1) * hd], k, nt_dims, preferred_element_type=F32) * scale
        p = jnp.exp(s - jnp.max(s, axis=-1, keepdims=True))
        p = p / jnp.sum(p, axis=-1, keepdims=True)
        heads.append(jnp.dot(p.astype(BF16), v, preferred_element_type=F32).astype(BF16))
    o = jnp.concatenate(heads, axis=-1)
    x2 = x + jnp.dot(o, wo_ref[...], preferred_element_type=F32)
    x_out_ref[...] = x2
    n_out_ref[...] = _rms(x2, gf_ref[...]).astype(n_out_ref.dtype)


def _cross(x, kv, wq, wo, layer, g_cross, g_ffn, mem_block_of_tile, tm):
    t, d = x.shape
    m = kv.shape[0] // mem_block_of_tile.n_seq
    mw = wq.shape[2]
    nbytes = (2 * (tm * d * (4 + 4 + 2) + m * 2 * mw * 2 + 2 * d * mw * 2 + 2 * d * 4)
              + 6 * tm * d * 4)
    return pl.pallas_call(
        _cross_kernel,
        grid=(t // tm,),
        in_specs=[pl.BlockSpec((tm, d), lambda i: (i, 0)),
                  pl.BlockSpec((m, 2 * mw), lambda i: (mem_block_of_tile(i), 0)),
                  pl.BlockSpec((None, d, mw), lambda i: (layer, 0, 0)),
                  pl.BlockSpec((None, mw, d), lambda i: (layer, 0, 0)),
                  pl.BlockSpec((1, d), lambda i: (0, 0)),
                  pl.BlockSpec((1, d), lambda i: (0, 0))],
        out_specs=[pl.BlockSpec((tm, d), lambda i: (i, 0)),
                   pl.BlockSpec((tm, d), lambda i: (i, 0))],
        out_shape=[jax.ShapeDtypeStruct((t, d), F32), jax.ShapeDtypeStruct((t, d), BF16)],
        compiler_params=_params(("parallel",), nbytes),
        name="cross",
    )(x, kv, wq, wo, g_cross.reshape(1, d), g_ffn.reshape(1, d))


class _SeqLayout:
    def __init__(self, n_a, len_a, n_b, len_b):
        self.n_a, self.len_a, self.n_b, self.len_b = n_a, len_a, n_b, len_b
        self.rows_a = n_a * len_a
        self.n_seq = n_a + n_b

    def seq_of_row(self, row):
        return jnp.where(row < self.rows_a, row // self.len_a,
                         self.n_a + (row - self.rows_a) // self.len_b)

    def pos_in_seq(self, row):
        return jnp.where(row < self.rows_a, row % self.len_a, (row - self.rows_a) % self.len_b)

    def seq_len_at(self, row):
        return jnp.where(row < self.rows_a, self.len_a, self.len_b)


class _MemBlockOfTile:
    def __init__(self, layout, tm):
        self.layout, self.tm, self.n_seq = layout, tm, layout.n_seq

    def __call__(self, i):
        return self.layout.seq_of_row(i * self.tm)


def _gelu(a):
    return 0.5 * a * (1.0 + lax.erf(a * (2.0 ** -0.5)))


def _ffn_kernel(n_hbm, wa_ref, wu_ref, cwa_ref, cwu_ref, cba_ref, cbu_ref, wd_ref, x_hbm, g_ref,
                out0_hbm, out1_hbm, lhs_ref, h_ref, acc_ref, sems, *, layout, tm, tf, nf, final):
    outs = (out0_hbm, out1_hbm)
    i = pl.program_id(0)
    f = pl.program_id(1)
    row0 = pl.multiple_of(i * tm, tm)
    first_down = 1

    def residual_copy():
        return pltpu.make_async_copy(x_hbm.at[pl.ds(row0, tm), :], acc_ref, sems.at[0])

    def rows_copy(src_row, dst_row, rows, sem):
        return pltpu.make_async_copy(n_hbm.at[pl.ds(src_row, rows), :],
                                     lhs_ref.at[pl.ds(dst_row, rows), :], sems.at[sem])

    slabs = tf // LANES

    def conv(slab, cw, cb):
        lo = h_ref[slab, pl.ds(HALO - 1, tm), :]
        mid = h_ref[slab, pl.ds(HALO, tm), :]
        hi = h_ref[slab, pl.ds(HALO + 1, tm), :]
        return lo * cw[0:1, :] + mid * cw[1:2, :] + hi * cw[2:3, :] + cb

    def gated():
        out = []
        for s in range(slabs):
            lanes = slice(s * LANES, (s + 1) * LANES)
            a = conv(s, cwa_ref[:, lanes], cba_ref[:, lanes])
            u = conv(slabs + s, cwu_ref[:, lanes], cbu_ref[:, lanes])
            out.append((_gelu(a) * u).astype(BF16))
        return jnp.concatenate(out, axis=-1)

    def store_hidden(ha, hu):
        for s in range(slabs):
            lanes = slice(s * LANES, (s + 1) * LANES)
            h_ref[s] = ha[:, lanes]
            h_ref[slabs + s] = hu[:, lanes]

    @pl.when(f == 0)
    def _():
        residual_copy().start()
        pos = layout.pos_in_seq(row0)
        first = pos == 0
        last = pos + tm == layout.seq_len_at(row0)
        main = rows_copy(row0, HALO, tm, 1)
        main.start()
        zero = jnp.zeros((HALO, lhs_ref.shape[1]), lhs_ref.dtype)

        @pl.when(first)
        def _():
            lhs_ref[0:HALO, :] = zero

        @pl.when(jnp.logical_not(first))
        def _():
            prev = rows_copy(row0 - HALO, 0, HALO, 2)
            prev.start()
            prev.wait()

        @pl.when(last)
        def _():
            lhs_ref[HALO + tm:, :] = zero

        @pl.when(jnp.logical_not(last))
        def _():
            nxt = rows_copy(row0 + tm, HALO + tm, HALO, 3)
            nxt.start()
            nxt.wait()

        main.wait()

    @pl.when(f == first_down)
    def _():
        residual_copy().wait()

    def stages(gate_down, up):
        if gate_down:
            acc_ref[...] += jnp.dot(gated(), wd_ref[...], preferred_element_type=F32)
        if up:
            lhs = lhs_ref[...]
            ha = jnp.dot(lhs, wa_ref[...], preferred_element_type=F32)
            hu = jnp.dot(lhs, wu_ref[...], preferred_element_type=F32)
            store_hidden(ha, hu)

    pl.when(f == 0)(functools.partial(stages, False, True))
    if nf > 1:
        pl.when(jnp.logical_and(f >= 1, f < nf))(functools.partial(stages, True, True))
    pl.when(f == nf)(functools.partial(stages, True, False))

    def normalize_rows(store):
        def body(c, carry):
            r0 = pl.multiple_of(c * NORM_CHUNK, NORM_CHUNK)
            store(r0, _rms(acc_ref[pl.ds(r0, NORM_CHUNK), :], g_ref[...]))
            return carry

        lax.fori_loop(0, tm // NORM_CHUNK, body, 0)

    @pl.when(f == nf)
    def _():
        if final:
            ya_hbm, yb_hbm = outs

            def store(r0, y):
                acc_ref[pl.ds(r0, NORM_CHUNK), :] = y

            normalize_rows(store)

            @pl.when(row0 < layout.rows_a)
            def _():
                out = pltpu.make_async_copy(acc_ref, ya_hbm.at[pl.ds(row0, tm), :], sems.at[4])
                out.start()
                out.wait()

            @pl.when(row0 >= layout.rows_a)
            def _():
                out = pltpu.make_async_copy(acc_ref, yb_hbm.at[pl.ds(row0 - layout.rows_a, tm), :],
                                            sems.at[4])
                out.start()
                out.wait()
        else:
            x_hbm_out, n_hbm_out = outs
            out_x = pltpu.make_async_copy(acc_ref, x_hbm_out.at[pl.ds(row0, tm), :], sems.at[4])
            out_x.start()

            def store(r0, y):
                lhs_ref[pl.ds(pl.multiple_of(HALO + r0, HALO), NORM_CHUNK), :] = y.astype(BF16)

            normalize_rows(store)
            out_n = pltpu.make_async_copy(lhs_ref.at[pl.ds(HALO, tm), :],
                                          n_hbm_out.at[pl.ds(row0, tm), :], sems.at[5])
            out_n.start()
            out_x.wait()
            out_n.wait()


def _ffn(n, x, w_up, conv_w, conv_b, w_down, layer, layout, gain, final):
    t, d = x.shape
    ff = w_down.shape[1]
    tm = _tile(min(layout.len_a, layout.len_b), FFN_ROWS)
    tf = _tile(ff, FFN_COLS)
    assert layout.len_a % tm == 0 and layout.len_b % tm == 0 and tm % HALO == 0
    nf = ff // tf
    conv_b = conv_b.reshape(1, 2 * ff)

    def up_tile(f):
        return jnp.minimum(f, nf - 1)

    def gate_tile(f):
        return jnp.maximum(f - 1, 0)

    down_tile = gate_tile
    hidden_rows = tm + 2 * HALO
    nbytes = (2 * (2 * d * tf * 2 + tf * d * 2) + hidden_rows * (d * 2 + 2 * tf * 4)
              + tm * d * 4 + 4 * hidden_rows * tf * 4 + 3 * tm * tf * 4)
    assert tm % NORM_CHUNK == 0
    if final:
        rows_b = t - layout.rows_a
        out_shape = [jax.ShapeDtypeStruct((layout.rows_a, d), F32), jax.ShapeDtypeStruct((rows_b, d), F32)]
    else:
        out_shape = [jax.ShapeDtypeStruct((t, d), F32), jax.ShapeDtypeStruct((t, d), BF16)]
    return pl.pallas_call(
        functools.partial(_ffn_kernel, layout=layout, tm=tm, tf=tf, nf=nf, final=final),
        grid=(t // tm, nf + 1),
        in_specs=[pl.BlockSpec(memory_space=pl.ANY),
                  pl.BlockSpec((None, d, tf), lambda i, f: (layer, 0, up_tile(f))),
                  pl.BlockSpec((None, d, tf), lambda i, f: (layer, 0, nf + up_tile(f))),
                  pl.BlockSpec((3, tf), lambda i, f: (0, gate_tile(f))),
                  pl.BlockSpec((3, tf), lambda i, f: (0, nf + gate_tile(f))),
                  pl.BlockSpec((1, tf), lambda i, f: (0, gate_tile(f))),
                  pl.BlockSpec((1, tf), lambda i, f: (0, nf + gate_tile(f))),
                  pl.BlockSpec((None, tf, d), lambda i, f: (layer, down_tile(f), 0)),
                  pl.BlockSpec(memory_space=pl.ANY),
                  pl.BlockSpec((1, d), lambda i, f: (0, 0))],
        out_specs=[pl.BlockSpec(memory_space=pl.ANY), pl.BlockSpec(memory_space=pl.ANY)],
        out_shape=out_shape,
        scratch_shapes=[pltpu.VMEM((hidden_rows, d), BF16),
                        pltpu.VMEM((2 * tf // LANES, hidden_rows, LANES), F32),
                        pltpu.VMEM((tm, d), F32),
                        pltpu.SemaphoreType.DMA((6,))],
        compiler_params=_params(("arbitrary", "arbitrary"), nbytes),
        name="ffn",
    )(n, w_up, w_up, conv_w, conv_w, conv_b, conv_b, w_down, x, gain.reshape(1, d))


def _rope_tables(layout):
    pos = jnp.concatenate([jnp.tile(jnp.arange(layout.len_a), layout.n_a),
                           jnp.tile(jnp.arange(layout.len_b), layout.n_b)])
    r = (pos // GRID_W).astype(F32)
    c = (pos % GRID_W).astype(F32)
    half = HEAD_DIM // 2
    freqs = ROPE_THETA ** (-jnp.arange(0, half, 2, dtype=F32) / half)
    ang_r = r[:, None] * freqs
    ang_c = c[:, None] * freqs
    cos = jnp.concatenate([jnp.cos(ang_r)] * 2 + [jnp.cos(ang_c)] * 2, axis=-1)
    sin = jnp.concatenate([-jnp.sin(ang_r), jnp.sin(ang_r), -jnp.sin(ang_c), jnp.sin(ang_c)], axis=-1)
    return cos, sin


def kernel(x_prompt, x_sample, mem_prompt, mem_sample, w_in, w_out, ret_decay_logit, q_norm, k_norm, norm_mix, norm_cross, norm_mem, w_cq, w_ckv, w_co, norm_ffn, w_up, conv_w, conv_b, w_down, norm_final):
    b, s, d = x_prompt.shape
    db, ds, _ = x_sample.shape
    m = mem_prompt.shape[1]
    depth = w_in.shape[0]
    layout = _SeqLayout(b, s, db, ds)
    t1, t2 = b * s, db * ds
    t = t1 + t2
    rw = RET_HEADS * HEAD_DIM
    aqw = ATT_Q_HEADS * HEAD_DIM
    akw = ATT_KV_HEADS * HEAD_DIM
    assert w_in.shape[2] == 4 * rw + aqw + 2 * akw and rw + aqw == w_out.shape[1]

    x = jnp.concatenate([x_prompt.reshape(t1, d), x_sample.reshape(t2, d)], axis=0)
    mem = jnp.concatenate([mem_prompt.reshape(b * m, d), mem_sample.reshape(db * m, d)], axis=0)
    cos, sin = _rope_tables(layout)
    log_g = jax.nn.log_sigmoid(ret_decay_logit.astype(F32))

    tm = _tile(t, ROW_TILE)
    tn = _tile(rw, COL_TILE)
    rope_extras = [(cos, (tm, HEAD_DIM), lambda j, i: (i, 0)),
                   (sin, (tm, HEAD_DIM), lambda j, i: (i, 0))]
    gqa_tk = _tile(min(s, ds), GQA_KV_ROWS)
    cross_tm = _tile(min(s, ds), CROSS_ROWS)
    mem_block = _MemBlockOfTile(layout, cross_tm)

    wi, wo = w_in.astype(BF16), w_out.astype(BF16)
    wcq, wckv, wco = w_cq.astype(BF16), w_ckv.astype(BF16), w_co.astype(BF16)
    wup, wdn = w_up.astype(BF16), w_down.astype(BF16)

    n = _rmsnorm(x, norm_mix[0], BF16)
    for l in range(depth):
        qk = _proj_deferred(n, wi, l, 0, 2 * rw, tn, BF16,
                            functools.partial(_epi_ret_qk, k_block0=rw // tn, k_scale=HEAD_DIM ** -0.5),
                            rope_extras)
        rv = _proj(n, wi, l, 2 * rw, rw, tn, BF16, _epi_plain)
        gate = _proj(n, wi, l, 3 * rw, rw, tn, F32, _epi_silu)
        aq = _proj_deferred(n, wi, l, 4 * rw, aqw, _tile(aqw, COL_TILE), BF16, _epi_att_q,
                            rope_extras + [(q_norm[l].reshape(1, HEAD_DIM), (1, HEAD_DIM),
                                            lambda j, i: (0, 0))])
        ak = _proj_deferred(n, wi, l, 4 * rw + aqw, akw, akw, BF16, _epi_att_k,
                            rope_extras + [(k_norm[l].reshape(1, HEAD_DIM), (1, HEAD_DIM),
                                            lambda j, i: (0, 0))])
        avt = _proj(n, wi, l, 4 * rw + aqw + akw, akw, akw, BF16, _epi_transposed_chunks,
                    transposed_chunk=gqa_tk)
        ret = jnp.concatenate([_retention(qk, rv, gate, log_g[l], 0, b, s),
                               _retention(qk, rv, gate, log_g[l], t1, db, ds)], axis=0)
        att = jnp.concatenate([_gqa(aq, ak, avt, 0, b, s), _gqa(aq, ak, avt, t1, db, ds)], axis=0)
        x = _out_proj(ret, att, wo, l, x)

        memn = _rmsnorm(mem, norm_mem[l], BF16)
        kv = _proj(memn, wckv, l, 0, w_ckv.shape[2], _tile(w_ckv.shape[2], COL_TILE),
                   BF16, _epi_plain, tm_pref=NORM_ROWS)
        x, n = _cross(x, kv, wcq, wco, l, norm_cross[l], norm_ffn[l], mem_block, cross_tm)
        if l + 1 < depth:
            x, n = _ffn(n, x, wup, conv_w[l], conv_b[l], wdn, l, layout, norm_mix[l + 1], final=False)
        else:
            y_a, y_b = _ffn(n, x, wup, conv_w[l], conv_b[l], wdn, l, layout, norm_final, final=True)

    return (y_a.reshape(b, s, d), y_b.reshape(db, ds, d))
```

```python
import functools

import jax
import jax.numpy as jnp
from jax import lax
from jax.experimental import pallas as pl
from jax.experimental.pallas import tpu as pltpu

F32 = jnp.float32
BF16 = jnp.bfloat16

HEAD_DIM = 128
RET_HEADS = 16
ATT_Q_HEADS = 16
ATT_KV_HEADS = 4
RET_CHUNK = 128
GRID_W = 64
ROPE_THETA = 10000.0
MEM_HEADS = 4
MEM_HEAD_DIM = 128
EPS = 1e-6
LOG2E = 1.4426950408889634

V7X_VMEM_BYTES = 64 * 1024 * 1024
V7X_VMEM_CAP = 60000 * 1024
LANES = 128
BF16_SUBLANES = 16

ROW_TILE = 1024
COL_TILE = 1024
NORM_ROWS = 256
CROSS_ROWS = 256
GQA_Q_ROWS = 512
GQA_KV_ROWS = 1024
GQA_COL_BLOCK = 512
RET_UNROLL = 16
FFN_ROWS = 1024
FFN_COLS = 256
FFN_UP_CHUNKS = 8
HALO = BF16_SUBLANES
NORM_CHUNK = 64


def _tile(n, pref):
    if n <= pref:
        return n
    t = pref
    while n % t:
        t //= 2
    return t


def _params(sem, vmem_bytes):
    limit = min(int(vmem_bytes), V7X_VMEM_CAP)
    return pltpu.CompilerParams(dimension_semantics=sem, vmem_limit_bytes=limit)


def _rms(x, g):
    y = x * lax.rsqrt(jnp.mean(x * x, axis=-1, keepdims=True) + EPS)
    return y * g


def _rmsnorm_kernel(x_ref, g_ref, o_ref):
    o_ref[...] = _rms(x_ref[...], g_ref[...]).astype(o_ref.dtype)


def _rmsnorm(x, g, out_dtype, row0=0, rows=None):
    d = x.shape[1]
    rows = x.shape[0] if rows is None else rows
    tm = _tile(rows, NORM_ROWS)
    assert row0 % tm == 0
    blk0 = row0 // tm
    nbytes = 2 * tm * d * (4 + jnp.dtype(out_dtype).itemsize) + 3 * tm * d * 4
    return pl.pallas_call(
        _rmsnorm_kernel,
        grid=(rows // tm,),
        in_specs=[pl.BlockSpec((tm, d), lambda i: (blk0 + i, 0)),
                  pl.BlockSpec((1, d), lambda i: (0, 0))],
        out_specs=pl.BlockSpec((tm, d), lambda i: (i, 0)),
        out_shape=jax.ShapeDtypeStruct((rows, d), out_dtype),
        compiler_params=_params(("parallel",), nbytes),
        name="rmsnorm",
    )(x, g.reshape(1, d))


def _rope(a, cos, sin, low_half):
    partner = jnp.where(low_half, pltpu.roll(a, 96, 1), pltpu.roll(a, 32, 1))
    return a * cos + partner * sin


def _low_half_mask(rows):
    lane = lax.broadcasted_iota(jnp.int32, (rows, HEAD_DIM), 1)
    return (lane & 32) == 0


def _proj_kernel(n_ref, w_ref, *rest, epilogue):
    acc = jnp.dot(n_ref[...], w_ref[...], preferred_element_type=F32)
    epilogue(acc, *rest)


def _epi_plain(acc, o_ref):
    o_ref[...] = acc.astype(o_ref.dtype)


def _epi_silu(acc, o_ref):
    o_ref[...] = jax.nn.silu(acc).astype(o_ref.dtype)


def _epi_ret_qk(acc, cos_ref, sin_ref, o_ref, *, k_block0, k_scale):
    j = pl.program_id(0)
    scale = jnp.where(j >= k_block0, k_scale, 1.0).astype(F32)
    cos, sin = cos_ref[...], sin_ref[...]
    low = _low_half_mask(acc.shape[0])
    for h in range(acc.shape[1] // HEAD_DIM):
        sl = slice(h * HEAD_DIM, (h + 1) * HEAD_DIM)
        o_ref[:, sl] = (_rope(acc[:, sl], cos, sin, low) * scale).astype(o_ref.dtype)


def _norm_rope_heads(acc, cos, sin, g, o_ref, scale=None):
    low = _low_half_mask(acc.shape[0])
    for h in range(acc.shape[1] // HEAD_DIM):
        sl = slice(h * HEAD_DIM, (h + 1) * HEAD_DIM)
        y = _rope(_rms(acc[:, sl], g), cos, sin, low)
        o_ref[:, sl] = (y if scale is None else y * scale).astype(o_ref.dtype)


def _epi_att_q(acc, cos_ref, sin_ref, g_ref, o_ref):
    _norm_rope_heads(acc, cos_ref[...], sin_ref[...], g_ref[...], o_ref, scale=HEAD_DIM ** -0.5 * LOG2E)


def _epi_att_k(acc, cos_ref, sin_ref, g_ref, o_ref):
    _norm_rope_heads(acc, cos_ref[...], sin_ref[...], g_ref[...], o_ref)


def _epi_transposed_chunks(acc, o_ref):
    tk = o_ref.shape[2]
    for c in range(o_ref.shape[0]):
        o_ref[c] = acc[c * tk:(c + 1) * tk, :].T.astype(o_ref.dtype)


def _proj(n, w, layer, col0, width, tn, out_dtype, epilogue, extras=(), tm_pref=None,
          transposed_chunk=None):
    r, d = n.shape
    tm = _tile(r, ROW_TILE if tm_pref is None else tm_pref)
    assert col0 % tn == 0 and width % tn == 0
    cb0 = col0 // tn
    in_specs = [pl.BlockSpec((tm, d), lambda j, i: (i, 0)),
                pl.BlockSpec((None, d, tn), lambda j, i: (layer, 0, cb0 + j))]
    in_specs += [pl.BlockSpec(shape, imap) for _, shape, imap in extras]
    osize = jnp.dtype(out_dtype).itemsize
    nbytes = (2 * (tm * d * 2 + d * tn * 2 + tm * tn * osize) + 4 * tm * tn * 4
              + sum(2 * 4 * shape[0] * shape[1] for _, shape, _ in extras))
    if transposed_chunk is None:
        out_spec = pl.BlockSpec((tm, tn), lambda j, i: (i, j))
        out_shape = jax.ShapeDtypeStruct((r, width), out_dtype)
    else:
        tk = transposed_chunk
        assert tm % tk == 0
        out_spec = pl.BlockSpec((tm // tk, tn, tk), lambda j, i: (i, j, 0))
        out_shape = jax.ShapeDtypeStruct((r // tk, width, tk), out_dtype)
    return pl.pallas_call(
        functools.partial(_proj_kernel, epilogue=epilogue),
        grid=(width // tn, r // tm),
        in_specs=in_specs,
        out_specs=out_spec,
        out_shape=out_shape,
        compiler_params=_params(("parallel", "parallel"), nbytes),
        name="proj",
    )(n, w, *[a for a, _, _ in extras])


def _proj_deferred_kernel(n_ref, w_ref, *rest, epilogue, ni):
    *epi_refs, prev_ref = rest
    i = pl.program_id(1)

    def step(multiply, finish):
        if multiply:
            acc = jnp.dot(n_ref[...], w_ref[...], preferred_element_type=F32)
        if finish:
            epilogue(prev_ref, *epi_refs)
        if multiply:
            prev_ref[...] = acc

    pl.when(i == 0)(functools.partial(step, True, False))
    if ni > 1:
        pl.when(jnp.logical_and(i > 0, i < ni))(functools.partial(step, True, True))
    pl.when(i == ni)(functools.partial(step, False, True))


def _proj_deferred(n, w, layer, col0, width, tn, out_dtype, epilogue, extras=()):
    r, d = n.shape
    tm = _tile(r, ROW_TILE)
    assert col0 % tn == 0 and width % tn == 0
    cb0 = col0 // tn
    ni = r // tm

    def mul_tile(i):
        return jnp.minimum(i, ni - 1)

    def epi_tile(i):
        return jnp.maximum(i - 1, 0)

    in_specs = [pl.BlockSpec((tm, d), lambda j, i: (mul_tile(i), 0)),
                pl.BlockSpec((None, d, tn), lambda j, i: (layer, 0, cb0 + j))]
    in_specs += [pl.BlockSpec(shape, functools.partial(lambda imap, j, i: imap(j, epi_tile(i)), imap))
                 for _, shape, imap in extras]
    osize = jnp.dtype(out_dtype).itemsize
    nbytes = (2 * (tm * d * 2 + d * tn * 2 + tm * tn * osize) + 5 * tm * tn * 4
              + sum(2 * 4 * shape[0] * shape[1] for _, shape, _ in extras))
    return pl.pallas_call(
        functools.partial(_proj_deferred_kernel, epilogue=epilogue, ni=ni),
        grid=(width // tn, ni + 1),
        in_specs=in_specs,
        out_specs=pl.BlockSpec((tm, tn), lambda j, i: (epi_tile(i), j)),
        out_shape=jax.ShapeDtypeStruct((r, width), out_dtype),
        scratch_shapes=[pltpu.VMEM((tm, tn), F32)],
        compiler_params=_params(("parallel", "arbitrary"), nbytes),
        name="proj_deferred",
    )(n, w, *[a for a, _, _ in extras])


def _retention_kernel(lg_ref, q_ref, k_ref, v_ref, g_ref, o_ref, bw_ref, *, nc):
    c_len = RET_CHUNK
    h = pl.program_id(1)
    lgf = lg_ref[0, h]
    lgb = lg_ref[1, h]
    ii = lax.broadcasted_iota(jnp.int32, (c_len, c_len), 0).astype(F32)
    jj = lax.broadcasted_iota(jnp.int32, (c_len, c_len), 1).astype(F32)
    diff = ii - jj
    decay = jnp.where(diff >= 0, jnp.exp(jnp.maximum(diff, 0.0) * lgf),
                      jnp.exp(jnp.maximum(-diff, 0.0) * lgb))
    ci = lax.broadcasted_iota(jnp.int32, (c_len, 1), 0).astype(F32)
    xi_f = jnp.exp((ci + 1.0) * lgf)
    zeta_f = jnp.exp((c_len - 1.0 - ci) * lgf)
    xi_b = jnp.exp((c_len - ci) * lgb)
    zeta_b = jnp.exp(ci * lgb)
    one = jnp.ones((1, 1), F32)
    chunk_f = jnp.exp(one * (c_len * lgf))
    chunk_b = jnp.exp(one * (c_len * lgb))
    tn_dims = (((0,), (0,)), ((), ()))
    nt_dims = (((1,), (1,)), ((), ()))

    def rows_of(c):
        return pl.ds(pl.multiple_of(c * c_len, c_len), c_len)

    group = RET_UNROLL if nc % RET_UNROLL == 0 else 1

    def state_chain(state, kvs, chunk_decay):
        before = []
        for kv in kvs:
            before.append(state)
            state = state * chunk_decay + kv
        return before, state

    def bw_body(t, state):
        rows = [rows_of(nc - 1 - (t * group + u)) for u in range(group)]
        kvs = [lax.dot_general((k_ref[r, :].astype(F32) * zeta_b).astype(BF16), v_ref[r, :], tn_dims,
                               preferred_element_type=F32) for r in rows]
        before, state = state_chain(state, kvs, chunk_b)
        for r, s_in in zip(rows, before):
            qx = (q_ref[r, :].astype(F32) * xi_b).astype(BF16)
            bw_ref[r, :] = jnp.dot(qx, s_in.astype(BF16), preferred_element_type=F32)
        return state

    lax.fori_loop(0, nc // group, bw_body, jnp.zeros((HEAD_DIM, HEAD_DIM), F32))

    def fw_body(t, state):
        rows = [rows_of(t * group + u) for u in range(group)]
        scores = [lax.dot_general(q_ref[r, :], k_ref[r, :], nt_dims, preferred_element_type=F32)
                  for r in rows]
        kvs = [lax.dot_general((k_ref[r, :].astype(F32) * zeta_f).astype(BF16), v_ref[r, :], tn_dims,
                               preferred_element_type=F32) for r in rows]
        before, state = state_chain(state, kvs, chunk_f)
        inters = [jnp.dot((q_ref[r, :].astype(F32) * xi_f).astype(BF16), s_in.astype(BF16),
                          preferred_element_type=F32) for r, s_in in zip(rows, before)]
        for r, s, inter in zip(rows, scores, inters):
            intra = jnp.dot((s * decay).astype(BF16), v_ref[r, :], preferred_element_type=F32)
            y = (intra + inter) + bw_ref[r, :]
            y = y * lax.rsqrt(jnp.mean(y * y, axis=-1, keepdims=True) + EPS)
            o_ref[r, :] = (g_ref[r, :] * y).astype(o_ref.dtype)
        return state

    lax.fori_loop(0, nc // group, fw_body, jnp.zeros((HEAD_DIM, HEAD_DIM), F32))


def _retention(qk, v, gate, log_g, row0, batch, seq):
    assert row0 % seq == 0 and seq % RET_CHUNK == 0
    sb0 = row0 // seq
    hd = HEAD_DIM
    nbytes = 2 * seq * hd * (2 + 2 + 2 + 4 + 2) + seq * hd * 4 + 64 * hd * hd * 4
    return pl.pallas_call(
        functools.partial(_retention_kernel, nc=seq // RET_CHUNK),
        grid=(batch, RET_HEADS),
        in_specs=[pl.BlockSpec(memory_space=pltpu.SMEM),
                  pl.BlockSpec((seq, hd), lambda b, h: (sb0 + b, h)),
                  pl.BlockSpec((seq, hd), lambda b, h: (sb0 + b, RET_HEADS + h)),
                  pl.BlockSpec((seq, hd), lambda b, h: (sb0 + b, h)),
                  pl.BlockSpec((seq, hd), lambda b, h: (sb0 + b, h))],
        out_specs=pl.BlockSpec((seq, hd), lambda b, h: (b, h)),
        out_shape=jax.ShapeDtypeStruct((batch * seq, RET_HEADS * hd), BF16),
        scratch_shapes=[pltpu.VMEM((seq, hd), F32)],
        compiler_params=_params(("parallel", "parallel"), nbytes),
        name="retention",
    )(log_g, qk, qk, v, gate)


def _gqa_kernel(q_ref, k_ref, vt_ref, o_ref, qs_ref, m_ref, l_ref, acc_ref, st_ref, *, nk, tq, group, cb):
    hd = HEAD_DIM
    r = group * tq
    for g in range(group):
        qs_ref[g * tq:(g + 1) * tq, :] = q_ref[:, g * hd:(g + 1) * hd]
    m_ref[...] = jnp.full(m_ref.shape, -jnp.inf, F32)
    l_ref[...] = jnp.zeros(l_ref.shape, F32)
    acc_ref[...] = jnp.zeros(acc_ref.shape, F32)
    nt_dims = (((1,), (1,)), ((), ()))

    nb = r // cb

    def scores(c, j):
        return lax.dot_general(k_ref[c], qs_ref[j * cb:(j + 1) * cb, :], nt_dims,
                               preferred_element_type=F32)

    st_ref[...] = scores(0, 0)

    def body(c, carry):
        vt = vt_ref[c]
        st = st_ref[...]
        for j in range(nb):
            cols = slice(j * cb, (j + 1) * cb)
            if j + 1 < nb:
                st_next = scores(c, j + 1)
            else:
                st_next = scores(jnp.minimum(c + 1, nk - 1), 0)
            m_prev = m_ref[:, cols]
            m_new = jnp.maximum(m_prev, jnp.max(st, axis=0, keepdims=True))
            alpha = jnp.exp2(m_prev - m_new)
            p = jnp.exp2(st - m_new)
            l_ref[:, cols] = alpha * l_ref[:, cols] + jnp.sum(p, axis=0, keepdims=True)
            acc_ref[:, cols] = alpha * acc_ref[:, cols] + jnp.dot(vt, p.astype(BF16),
                                                                  preferred_element_type=F32)
            m_ref[:, cols] = m_new
            st = st_next
        st_ref[...] = st
        return carry

    lax.fori_loop(0, nk, body, 0)
    out = (acc_ref[...] / l_ref[...]).T
    for g in range(group):
        o_ref[:, g * hd:(g + 1) * hd] = out[g * tq:(g + 1) * tq, :].astype(o_ref.dtype)


def _gqa(aq, ak, avt, row0, batch, seq):
    hd = HEAD_DIM
    group = ATT_Q_HEADS // ATT_KV_HEADS
    tq = _tile(seq, GQA_Q_ROWS)
    tk = avt.shape[2]
    assert row0 % seq == 0 and seq % tk == 0
    sb0 = row0 // seq
    qb0 = row0 // tq
    nq = seq // tq
    nk = seq // tk
    gw = group * hd
    r = group * tq
    cb = _tile(r, GQA_COL_BLOCK)
    ak3 = ak.reshape(ak.shape[0] // tk, tk, ak.shape[1])
    nbytes = (2 * (2 * tq * gw * 2 + 2 * seq * hd * 2) + r * hd * (2 + 4) + 2 * 8 * r * 4
              + 6 * tk * cb * 4 + 2 * r * hd * 4)
    return pl.pallas_call(
        functools.partial(_gqa_kernel, nk=nk, tq=tq, group=group, cb=cb),
        grid=(batch, ATT_KV_HEADS, nq),
        in_specs=[pl.BlockSpec((tq, gw), lambda b, h, i: (qb0 + b * nq + i, h)),
                  pl.BlockSpec((nk, tk, hd), lambda b, h, i: (sb0 + b, 0, h)),
                  pl.BlockSpec((nk, hd, tk), lambda b, h, i: (sb0 + b, h, 0))],
        out_specs=pl.BlockSpec((tq, gw), lambda b, h, i: (b * nq + i, h)),
        out_shape=jax.ShapeDtypeStruct((batch * seq, ATT_Q_HEADS * hd), BF16),
        scratch_shapes=[pltpu.VMEM((r, hd), BF16),
                        pltpu.VMEM((1, r), F32),
                        pltpu.VMEM((1, r), F32),
                        pltpu.VMEM((hd, r), F32),
                        pltpu.VMEM((tk, cb), F32)],
        compiler_params=_params(("parallel", "parallel", "parallel"), nbytes),
        name="gqa",
    )(aq, ak3, avt)


def _out_proj_kernel(ret_ref, att_ref, w_ref, x_ref, o_ref, *, rw):
    acc = jnp.dot(ret_ref[...], w_ref[:rw, :], preferred_element_type=F32)
    acc = acc + jnp.dot(att_ref[...], w_ref[rw:, :], preferred_element_type=F32)
    o_ref[...] = x_ref[...] + acc


def _out_proj(ret, att, w, layer, x):
    t, d = x.shape
    rw, aw = ret.shape[1], att.shape[1]
    tm = _tile(t, ROW_TILE)
    tn = _tile(d, COL_TILE)
    nbytes = 2 * (tm * (rw + aw) * 2 + (rw + aw) * tn * 2 + 2 * tm * tn * 4) + 3 * tm * tn * 4
    return pl.pallas_call(
        functools.partial(_out_proj_kernel, rw=rw),
        grid=(d // tn, t // tm),
        in_specs=[pl.BlockSpec((tm, rw), lambda j, i: (i, 0)),
                  pl.BlockSpec((tm, aw), lambda j, i: (i, 0)),
                  pl.BlockSpec((None, rw + aw, tn), lambda j, i: (layer, 0, j)),
                  pl.BlockSpec((tm, tn), lambda j, i: (i, j))],
        out_specs=pl.BlockSpec((tm, tn), lambda j, i: (i, j)),
        out_shape=jax.ShapeDtypeStruct((t, d), F32),
        compiler_params=_params(("parallel", "parallel"), nbytes),
        name="out_proj",
    )(ret, att, w, x)


def _cross_kernel(x_ref, kv_ref, wq_ref, wo_ref, gc_ref, gf_ref, x_out_ref, n_out_ref):
    hd = MEM_HEAD_DIM
    mw = MEM_HEADS * hd
    x = x_ref[...]
    n = _rms(x, gc_ref[...]).astype(BF16)
    q = jnp.dot(n, wq_ref[...], preferred_element_type=F32).astype(BF16)
    scale = hd ** -0.5
    nt_dims = (((1,), (1,)), ((), ()))
    heads = []
    for h in range(MEM_HEADS):
        k = kv_ref[:, h * hd:(h + 1) * hd]
        v = kv_ref[:, mw + h * hd:mw + (h + 1) * hd]
        s = lax.dot_general(q[:, h * hd:(h + 1) * hd], k, nt_dims, preferred_element_type=F32) * scale
        p = jnp.exp(s - jnp.max(s, axis=-1, keepdims=True))
        p = p / jnp.sum(p, axis=-1, keepdims=True)
        heads.append(jnp.dot(p.astype(BF16), v, preferred_element_type=F32).astype(BF16))
    o = jnp.concatenate(heads, axis=-1)
    x2 = x + jnp.dot(o, wo_ref[...], preferred_element_type=F32)
    x_out_ref[...] = x2
    n_out_ref[...] = _rms(x2, gf_ref[...]).astype(n_out_ref.dtype)


def _cross(x, kv, wq, wo, layer, g_cross, g_ffn, mem_block_of_tile, tm):
    t, d = x.shape
    m = kv.shape[0] // mem_block_of_tile.n_seq
    mw = wq.shape[2]
    nbytes = (2 * (tm * d * (4 + 4 + 2) + m * 2 * mw * 2 + 2 * d * mw * 2 + 2 * d * 4)
              + 6 * tm * d * 4)
    return pl.pallas_call(
        _cross_kernel,
        grid=(t // tm,),
        in_specs=[pl.BlockSpec((tm, d), lambda i: (i, 0)),
                  pl.BlockSpec((m, 2 * mw), lambda i: (mem_block_of_tile(i), 0)),
                  pl.BlockSpec((None, d, mw), lambda i: (layer, 0, 0)),
                  pl.BlockSpec((None, mw, d), lambda i: (layer, 0, 0)),
                  pl.BlockSpec((1, d), lambda i: (0, 0)),
                  pl.BlockSpec((1, d), lambda i: (0, 0))],
        out_specs=[pl.BlockSpec((tm, d), lambda i: (i, 0)),
                   pl.BlockSpec((tm, d), lambda i: (i, 0))],
        out_shape=[jax.ShapeDtypeStruct((t, d), F32), jax.ShapeDtypeStruct((t, d), BF16)],
        compiler_params=_params(("parallel",), nbytes),
        name="cross",
    )(x, kv, wq, wo, g_cross.reshape(1, d), g_ffn.reshape(1, d))


class _SeqLayout:
    def __init__(self, n_a, len_a, n_b, len_b):
        self.n_a, self.len_a, self.n_b, self.len_b = n_a, len_a, n_b, len_b
        self.rows_a = n_a * len_a
        self.n_seq = n_a + n_b

    def seq_of_row(self, row):
        return jnp.where(row < self.rows_a, row // self.len_a,
                         self.n_a + (row - self.rows_a) // self.len_b)

    def pos_in_seq(self, row):
        return jnp.where(row < self.rows_a, row % self.len_a, (row - self.rows_a) % self.len_b)

    def seq_len_at(self, row):
        return jnp.where(row < self.rows_a, self.len_a, self.len_b)


class _MemBlockOfTile:
    def __init__(self, layout, tm):
        self.layout, self.tm, self.n_seq = layout, tm, layout.n_seq

    def __call__(self, i):
        return self.layout.seq_of_row(i * self.tm)


def _gelu(a):
    return 0.5 * a * (1.0 + lax.erf(a * (2.0 ** -0.5)))


def _ffn_kernel(n_hbm, wa_ref, wu_ref, cwa_ref, cwu_ref, cba_ref, cbu_ref, wd_ref, x_hbm, g_ref,
                out0_hbm, out1_hbm, lhs_ref, h_ref, act_ref, acc_ref, sems,
                *, layout, tm, tf, nf, final, up_chunks):
    outs = (out0_hbm, out1_hbm)
    i = pl.program_id(0)
    f = pl.program_id(1)
    row0 = pl.multiple_of(i * tm, tm)
    first_down = 2
    last_step = nf + 1

    def residual_copy():
        return pltpu.make_async_copy(x_hbm.at[pl.ds(row0, tm), :], acc_ref, sems.at[0])

    def rows_copy(src_row, dst_row, rows, sem):
        return pltpu.make_async_copy(n_hbm.at[pl.ds(src_row, rows), :],
                                     lhs_ref.at[pl.ds(dst_row, rows), :], sems.at[sem])

    slabs = tf // LANES

    def conv(slab, cw, cb):
        lo = h_ref[slab, pl.ds(HALO - 1, tm), :]
        mid = h_ref[slab, pl.ds(HALO, tm), :]
        hi = h_ref[slab, pl.ds(HALO + 1, tm), :]
        return lo * cw[0:1, :] + mid * cw[1:2, :] + hi * cw[2:3, :] + cb

    def gate_into_act():
        for s in range(slabs):
            lanes = slice(s * LANES, (s + 1) * LANES)
            a = conv(s, cwa_ref[:, lanes], cba_ref[:, lanes])
            u = conv(slabs + s, cwu_ref[:, lanes], cbu_ref[:, lanes])
            act_ref[:, lanes] = (_gelu(a) * u).astype(BF16)

    def up_project():
        for r0, r1 in up_chunks:
            lhs = lhs_ref[r0:r1, :]
            ha = jnp.dot(lhs, wa_ref[...], preferred_element_type=F32)
            hu = jnp.dot(lhs, wu_ref[...], preferred_element_type=F32)
            for s in range(slabs):
                lanes = slice(s * LANES, (s + 1) * LANES)
                h_ref[s, r0:r1, :] = ha[:, lanes]
                h_ref[slabs + s, r0:r1, :] = hu[:, lanes]

    @pl.when(f == 0)
    def _():
        residual_copy().start()
        pos = layout.pos_in_seq(row0)
        first = pos == 0
        last = pos + tm == layout.seq_len_at(row0)
        main = rows_copy(row0, HALO, tm, 1)
        main.start()
        zero = jnp.zeros((HALO, lhs_ref.shape[1]), lhs_ref.dtype)

        @pl.when(first)
        def _():
            lhs_ref[0:HALO, :] = zero

        @pl.when(jnp.logical_not(first))
        def _():
            prev = rows_copy(row0 - HALO, 0, HALO, 2)
            prev.start()
            prev.wait()

        @pl.when(last)
        def _():
            lhs_ref[HALO + tm:, :] = zero

        @pl.when(jnp.logical_not(last))
        def _():
            nxt = rows_copy(row0 + tm, HALO + tm, HALO, 3)
            nxt.start()
            nxt.wait()

        main.wait()

    @pl.when(f == first_down)
    def _():
        residual_copy().wait()

    def stages(down, gate, up):
        if down:
            acc_ref[...] += jnp.dot(act_ref[...], wd_ref[...], preferred_element_type=F32)
        if gate:
            gate_into_act()
        if up:
            up_project()

    pl.when(f == 0)(functools.partial(stages, False, False, True))
    pl.when(f == 1)(functools.partial(stages, False, True, nf > 1))
    if nf > 2:
        pl.when(jnp.logical_and(f >= 2, f < nf))(functools.partial(stages, True, True, True))
    if nf > 1:
        pl.when(f == nf)(functools.partial(stages, True, True, False))
    pl.when(f == last_step)(functools.partial(stages, True, False, False))

    def normalize_rows(store):
        def body(c, carry):
            r0 = pl.multiple_of(c * NORM_CHUNK, NORM_CHUNK)
            store(r0, _rms(acc_ref[pl.ds(r0, NORM_CHUNK), :], g_ref[...]))
            return carry

        lax.fori_loop(0, tm // NORM_CHUNK, body, 0)

    @pl.when(f == last_step)
    def _():
        if final:
            ya_hbm, yb_hbm = outs

            def store(r0, y):
                acc_ref[pl.ds(r0, NORM_CHUNK), :] = y

            normalize_rows(store)

            @pl.when(row0 < layout.rows_a)
            def _():
                out = pltpu.make_async_copy(acc_ref, ya_hbm.at[pl.ds(row0, tm), :], sems.at[4])
                out.start()
                out.wait()

            @pl.when(row0 >= layout.rows_a)
            def _():
                out = pltpu.make_async_copy(acc_ref, yb_hbm.at[pl.ds(row0 - layout.rows_a, tm), :],
                                            sems.at[4])
                out.start()
                out.wait()
        else:
            x_hbm_out, n_hbm_out = outs
            out_x = pltpu.make_async_copy(acc_ref, x_hbm_out.at[pl.ds(row0, tm), :], sems.at[4])
            out_x.start()

            def store(r0, y):
                lhs_ref[pl.ds(pl.multiple_of(HALO + r0, HALO), NORM_CHUNK), :] = y.astype(BF16)

            normalize_rows(store)
            out_n = pltpu.make_async_copy(lhs_ref.at[pl.ds(HALO, tm), :],
                                          n_hbm_out.at[pl.ds(row0, tm), :], sems.at[5])
            out_n.start()
            out_x.wait()
            out_n.wait()


def _ffn(n, x, w_up, conv_w, conv_b, w_down, layer, layout, gain, final):
    t, d = x.shape
    ff = w_down.shape[1]
    tm = _tile(min(layout.len_a, layout.len_b), FFN_ROWS)
    tf = _tile(ff, FFN_COLS)
    assert layout.len_a % tm == 0 and layout.len_b % tm == 0 and tm % HALO == 0
    nf = ff // tf
    conv_b = conv_b.reshape(1, 2 * ff)

    def up_tile(f):
        return jnp.minimum(f, nf - 1)

    def gate_tile(f):
        return jnp.clip(f - 1, 0, nf - 1)

    def down_tile(f):
        return jnp.clip(f - 2, 0, nf - 1)

    hidden_rows = tm + 2 * HALO
    n_chunks = min(FFN_UP_CHUNKS, hidden_rows // HALO)
    bounds = [HALO * ((hidden_rows // HALO) * c // n_chunks) for c in range(n_chunks + 1)]
    up_chunks = tuple(zip(bounds[:-1], bounds[1:]))
    nbytes =(2 * (2 * d * tf * 2 + tf * d * 2) + hidden_rows * (d * 2 + 2 * tf * 4)
              + tm * d * 4 + 4 * hidden_rows * tf * 4 + 3 * tm * tf * 4)
    assert tm % NORM_CHUNK == 0
    if final:
        rows_b = t - layout.rows_a
        out_shape = [jax.ShapeDtypeStruct((layout.rows_a, d), F32), jax.ShapeDtypeStruct((rows_b, d), F32)]
    else:
        out_shape = [jax.ShapeDtypeStruct((t, d), F32), jax.ShapeDtypeStruct((t, d), BF16)]
    return pl.pallas_call(
        functools.partial(_ffn_kernel, layout=layout, tm=tm, tf=tf, nf=nf, final=final,
                          up_chunks=up_chunks),
        grid=(t // tm, nf + 2),
        in_specs=[pl.BlockSpec(memory_space=pl.ANY),
                  pl.BlockSpec((None, d, tf), lambda i, f: (layer, 0, up_tile(f))),
                  pl.BlockSpec((None, d, tf), lambda i, f: (layer, 0, nf + up_tile(f))),
                  pl.BlockSpec((3, tf), lambda i, f: (0, gate_tile(f))),
                  pl.BlockSpec((3, tf), lambda i, f: (0, nf + gate_tile(f))),
                  pl.BlockSpec((1, tf), lambda i, f: (0, gate_tile(f))),
                  pl.BlockSpec((1, tf), lambda i, f: (0, nf + gate_tile(f))),
                  pl.BlockSpec((None, tf, d), lambda i, f: (layer, down_tile(f), 0)),
                  pl.BlockSpec(memory_space=pl.ANY),
                  pl.BlockSpec((1, d), lambda i, f: (0, 0))],
        out_specs=[pl.BlockSpec(memory_space=pl.ANY), pl.BlockSpec(memory_space=pl.ANY)],
        out_shape=out_shape,
        scratch_shapes=[pltpu.VMEM((hidden_rows, d), BF16),
                        pltpu.VMEM((2 * tf // LANES, hidden_rows, LANES), F32),
                        pltpu.VMEM((tm, tf), BF16),
                        pltpu.VMEM((tm, d), F32),
                        pltpu.SemaphoreType.DMA((6,))],
        compiler_params=_params(("arbitrary", "arbitrary"), nbytes),
        name="ffn",
    )(n, w_up, w_up, conv_w, conv_w, conv_b, conv_b, w_down, x, gain.reshape(1, d))


def _rope_tables(layout):
    pos = jnp.concatenate([jnp.tile(jnp.arange(layout.len_a), layout.n_a),
                           jnp.tile(jnp.arange(layout.len_b), layout.n_b)])
    r = (pos // GRID_W).astype(F32)
    c = (pos % GRID_W).astype(F32)
    half = HEAD_DIM // 2
    freqs = ROPE_THETA ** (-jnp.arange(0, half, 2, dtype=F32) / half)
    ang_r = r[:, None] * freqs
    ang_c = c[:, None] * freqs
    cos = jnp.concatenate([jnp.cos(ang_r)] * 2 + [jnp.cos(ang_c)] * 2, axis=-1)
    sin = jnp.concatenate([-jnp.sin(ang_r), jnp.sin(ang_r), -jnp.sin(ang_c), jnp.sin(ang_c)], axis=-1)
    return cos, sin


def kernel(x_prompt, x_sample, mem_prompt, mem_sample, w_in, w_out, ret_decay_logit, q_norm, k_norm, norm_mix, norm_cross, norm_mem, w_cq, w_ckv, w_co, norm_ffn, w_up, conv_w, conv_b, w_down, norm_final):
    b, s, d = x_prompt.shape
    db, ds, _ = x_sample.shape
    m = mem_prompt.shape[1]
    depth = w_in.shape[0]
    layout = _SeqLayout(b, s, db, ds)
    t1, t2 = b * s, db * ds
    t = t1 + t2
    rw = RET_HEADS * HEAD_DIM
    aqw = ATT_Q_HEADS * HEAD_DIM
    akw = ATT_KV_HEADS * HEAD_DIM
    assert w_in.shape[2] == 4 * rw + aqw + 2 * akw and rw + aqw == w_out.shape[1]

    x = jnp.concatenate([x_prompt.reshape(t1, d), x_sample.reshape(t2, d)], axis=0)
    mem = jnp.concatenate([mem_prompt.reshape(b * m, d), mem_sample.reshape(db * m, d)], axis=0)
    cos, sin = _rope_tables(layout)
    log_g = jax.nn.log_sigmoid(ret_decay_logit.astype(F32))

    tm = _tile(t, ROW_TILE)
    tn = _tile(rw, COL_TILE)
    rope_extras = [(cos, (tm, HEAD_DIM), lambda j, i: (i, 0)),
                   (sin, (tm, HEAD_DIM), lambda j, i: (i, 0))]
    gqa_tk = _tile(min(s, ds), GQA_KV_ROWS)
    cross_tm = _tile(min(s, ds), CROSS_ROWS)
    mem_block = _MemBlockOfTile(layout, cross_tm)

    wi, wo = w_in.astype(BF16), w_out.astype(BF16)
    wcq, wckv, wco = w_cq.astype(BF16), w_ckv.astype(BF16), w_co.astype(BF16)
    wup, wdn = w_up.astype(BF16), w_down.astype(BF16)

    n = _rmsnorm(x, norm_mix[0], BF16)
    for l in range(depth):
        qk = _proj_deferred(n, wi, l, 0, 2 * rw, tn, BF16,
                            functools.partial(_epi_ret_qk, k_block0=rw // tn, k_scale=HEAD_DIM ** -0.5),
                            rope_extras)
        rv = _proj(n, wi, l, 2 * rw, rw, tn, BF16, _epi_plain)
        gate = _proj(n, wi, l, 3 * rw, rw, tn, F32, _epi_silu)
        aq = _proj_deferred(n, wi, l, 4 * rw, aqw, _tile(aqw, COL_TILE), BF16, _epi_att_q,
                            rope_extras + [(q_norm[l].reshape(1, HEAD_DIM), (1, HEAD_DIM),
                                            lambda j, i: (0, 0))])
        ak = _proj_deferred(n, wi, l, 4 * rw + aqw, akw, akw, BF16, _epi_att_k,
                            rope_extras + [(k_norm[l].reshape(1, HEAD_DIM), (1, HEAD_DIM),
                                            lambda j, i: (0, 0))])
        avt = _proj(n, wi, l, 4 * rw + aqw + akw, akw, akw, BF16, _epi_transposed_chunks,
                    transposed_chunk=gqa_tk)
        ret = jnp.concatenate([_retention(qk, rv, gate, log_g[l], 0, b, s),
                               _retention(qk, rv, gate, log_g[l], t1, db, ds)], axis=0)
        att = jnp.concatenate([_gqa(aq, ak, avt, 0, b, s), _gqa(aq, ak, avt, t1, db, ds)], axis=0)
        x = _out_proj(ret, att, wo, l, x)

        memn = _rmsnorm(mem, norm_mem[l], BF16)
        kv = _proj(memn, wckv, l, 0, w_ckv.shape[2], _tile(w_ckv.shape[2], COL_TILE),
                   BF16, _epi_plain, tm_pref=NORM_ROWS)
        x, n = _cross(x, kv, wcq, wco, l, norm_cross[l], norm_ffn[l], mem_block, cross_tm)
        if l + 1 < depth:
            x, n = _ffn(n, x, wup, conv_w[l], conv_b[l], wdn, l, layout, norm_mix[l + 1], final=False)
        else:
            y_a, y_b = _ffn(n, x, wup, conv_w[l], conv_b[l], wdn, l, layout, norm_final, final=True)

    return (y_a.reshape(b, s, d), y_b.reshape(db, ds, d))
```

```python
import functools

import jax
import jax.numpy as jnp
from jax import lax
from jax.experimental import pallas as pl
from jax.experimental.pallas import tpu as pltpu

F32 = jnp.float32
BF16 = jnp.bfloat16

HEAD_DIM = 128
RET_HEADS = 16
ATT_Q_HEADS = 16
ATT_KV_HEADS = 4
RET_CHUNK = 128
GRID_W = 64
ROPE_THETA = 10000.0
MEM_HEADS = 4
MEM_HEAD_DIM = 128
EPS = 1e-6
LOG2E = 1.4426950408889634

V7X_VMEM_BYTES = 64 * 1024 * 1024
V7X_VMEM_CAP = 60000 * 1024
LANES = 128
BF16_SUBLANES = 16

ROW_TILE = 1024
COL_TILE = 1024
NORM_ROWS = 256
CROSS_ROWS = 256
OUT_PROJ_ROWS = 512
GQA_Q_ROWS = 512
GQA_KV_ROWS = 1024
GQA_COL_BLOCK = 512
RET_UNROLL = 16
FFN_ROWS = 1024
FFN_COLS = 256
FFN_UP_CHUNKS = 8
HALO = BF16_SUBLANES
NORM_CHUNK = 64


def _tile(n, pref):
    if n <= pref:
        return n
    t = pref
    while n % t:
        t //= 2
    return t


def _params(sem, vmem_bytes):
    limit = min(int(vmem_bytes), V7X_VMEM_CAP)
    return pltpu.CompilerParams(dimension_semantics=sem, vmem_limit_bytes=limit)


def _rms(x, g):
    y = x * lax.rsqrt(jnp.mean(x * x, axis=-1, keepdims=True) + EPS)
    return y * g


def _rmsnorm_kernel(x_ref, g_ref, o_ref):
    o_ref[...] = _rms(x_ref[...], g_ref[...]).astype(o_ref.dtype)


def _rmsnorm(x, g, out_dtype, row0=0, rows=None):
    d = x.shape[1]
    rows = x.shape[0] if rows is None else rows
    tm = _tile(rows, NORM_ROWS)
    assert row0 % tm == 0
    blk0 = row0 // tm
    nbytes = 2 * tm * d * (4 + jnp.dtype(out_dtype).itemsize) + 3 * tm * d * 4
    return pl.pallas_call(
        _rmsnorm_kernel,
        grid=(rows // tm,),
        in_specs=[pl.BlockSpec((tm, d), lambda i: (blk0 + i, 0)),
                  pl.BlockSpec((1, d), lambda i: (0, 0))],
        out_specs=pl.BlockSpec((tm, d), lambda i: (i, 0)),
        out_shape=jax.ShapeDtypeStruct((rows, d), out_dtype),
        compiler_params=_params(("parallel",), nbytes),
        name="rmsnorm",
    )(x, g.reshape(1, d))


def _rmsnorm_parts_kernel(a_ref, b_ref, g_ref, o_ref, *, na):
    x = _select_part(pl.program_id(0), na, a_ref, b_ref)
    o_ref[...] = _rms(x, g_ref[...]).astype(o_ref.dtype)


def _rmsnorm_parts(parts, g, out_dtype):
    d = parts[0].shape[1]
    t = parts[0].shape[0] + parts[1].shape[0]
    tm = _tile(min(parts[0].shape[0], parts[1].shape[0]), NORM_ROWS)
    specs, na = _part_specs(parts, tm, d, 0)
    nbytes = 2 * tm * d * (8 + jnp.dtype(out_dtype).itemsize) + 4 * tm * d * 4
    return pl.pallas_call(
        functools.partial(_rmsnorm_parts_kernel, na=na),
        grid=(t // tm,),
        in_specs=specs + [pl.BlockSpec((1, d), lambda i: (0, 0))],
        out_specs=pl.BlockSpec((tm, d), lambda i: (i, 0)),
        out_shape=jax.ShapeDtypeStruct((t, d), out_dtype),
        compiler_params=_params(("parallel",), nbytes),
        name="rmsnorm_parts",
    )(*parts, g.reshape(1, d))


def _rope(a, cos, sin, low_half):
    partner = jnp.where(low_half, pltpu.roll(a, 96, 1), pltpu.roll(a, 32, 1))
    return a * cos + partner * sin


def _low_half_mask(rows):
    lane = lax.broadcasted_iota(jnp.int32, (rows, HEAD_DIM), 1)
    return (lane & 32) == 0


def _proj_kernel(n_ref, w_ref, *rest, epilogue):
    acc = jnp.dot(n_ref[...], w_ref[...], preferred_element_type=F32)
    epilogue(acc, *rest)


def _epi_plain(acc, o_ref):
    o_ref[...] = acc.astype(o_ref.dtype)


def _epi_silu(acc, o_ref):
    o_ref[...] = jax.nn.silu(acc).astype(o_ref.dtype)


def _epi_ret_qk(acc, cos_ref, sin_ref, o_ref, *, k_block0, k_scale):
    j = pl.program_id(0)
    scale = jnp.where(j >= k_block0, k_scale, 1.0).astype(F32)
    cos, sin = cos_ref[...], sin_ref[...]
    low = _low_half_mask(acc.shape[0])
    for h in range(acc.shape[1] // HEAD_DIM):
        sl = slice(h * HEAD_DIM, (h + 1) * HEAD_DIM)
        o_ref[:, sl] = (_rope(acc[:, sl], cos, sin, low) * scale).astype(o_ref.dtype)


def _norm_rope_heads(acc, cos, sin, g, o_ref, scale=None):
    low = _low_half_mask(acc.shape[0])
    for h in range(acc.shape[1] // HEAD_DIM):
        sl = slice(h * HEAD_DIM, (h + 1) * HEAD_DIM)
        y = _rope(_rms(acc[:, sl], g), cos, sin, low)
        o_ref[:, sl] = (y if scale is None else y * scale).astype(o_ref.dtype)


def _epi_att_q(acc, cos_ref, sin_ref, g_ref, o_ref):
    _norm_rope_heads(acc, cos_ref[...], sin_ref[...], g_ref[...], o_ref, scale=HEAD_DIM ** -0.5 * LOG2E)


def _epi_att_k(acc, cos_ref, sin_ref, g_ref, o_ref):
    _norm_rope_heads(acc, cos_ref[...], sin_ref[...], g_ref[...], o_ref)


def _epi_transposed_chunks(acc, o_ref):
    tk = o_ref.shape[2]
    for c in range(o_ref.shape[0]):
        o_ref[c] = acc[c * tk:(c + 1) * tk, :].T.astype(o_ref.dtype)


def _proj(n, w, layer, col0, width, tn, out_dtype, epilogue, extras=(), tm_pref=None,
          transposed_chunk=None):
    r, d = n.shape
    tm = _tile(r, ROW_TILE if tm_pref is None else tm_pref)
    assert col0 % tn == 0 and width % tn == 0
    cb0 = col0 // tn
    in_specs = [pl.BlockSpec((tm, d), lambda j, i: (i, 0)),
                pl.BlockSpec((None, d, tn), lambda j, i: (layer, 0, cb0 + j))]
    in_specs += [pl.BlockSpec(shape, imap) for _, shape, imap in extras]
    osize = jnp.dtype(out_dtype).itemsize
    nbytes = (2 * (tm * d * 2 + d * tn * 2 + tm * tn * osize) + 4 * tm * tn * 4
              + sum(2 * 4 * shape[0] * shape[1] for _, shape, _ in extras))
    if transposed_chunk is None:
        out_spec = pl.BlockSpec((tm, tn), lambda j, i: (i, j))
        out_shape = jax.ShapeDtypeStruct((r, width), out_dtype)
    else:
        tk = transposed_chunk
        assert tm % tk == 0
        out_spec = pl.BlockSpec((tm // tk, tn, tk), lambda j, i: (i, j, 0))
        out_shape = jax.ShapeDtypeStruct((r // tk, width, tk), out_dtype)
    return pl.pallas_call(
        functools.partial(_proj_kernel, epilogue=epilogue),
        grid=(width // tn, r // tm),
        in_specs=in_specs,
        out_specs=out_spec,
        out_shape=out_shape,
        compiler_params=_params(("parallel", "parallel"), nbytes),
        name="proj",
    )(n, w, *[a for a, _, _ in extras])


def _proj_deferred_kernel(n_ref, w_ref, *rest, epilogue, ni):
    *epi_refs, prev_ref = rest
    i = pl.program_id(1)

    def step(multiply, finish):
        if multiply:
            acc = jnp.dot(n_ref[...], w_ref[...], preferred_element_type=F32)
        if finish:
            epilogue(prev_ref, *epi_refs)
        if multiply:
            prev_ref[...] = acc

    pl.when(i == 0)(functools.partial(step, True, False))
    if ni > 1:
        pl.when(jnp.logical_and(i > 0, i < ni))(functools.partial(step, True, True))
    pl.when(i == ni)(functools.partial(step, False, True))


def _proj_deferred(n, w, layer, col0, width, tn, out_dtype, epilogue, extras=()):
    r, d = n.shape
    tm = _tile(r, ROW_TILE)
    assert col0 % tn == 0 and width % tn == 0
    cb0 = col0 // tn
    ni = r // tm

    def mul_tile(i):
        return jnp.minimum(i, ni - 1)

    def epi_tile(i):
        return jnp.maximum(i - 1, 0)

    in_specs = [pl.BlockSpec((tm, d), lambda j, i: (mul_tile(i), 0)),
                pl.BlockSpec((None, d, tn), lambda j, i: (layer, 0, cb0 + j))]
    in_specs += [pl.BlockSpec(shape, functools.partial(lambda imap, j, i: imap(j, epi_tile(i)), imap))
                 for _, shape, imap in extras]
    osize = jnp.dtype(out_dtype).itemsize
    nbytes = (2 * (tm * d * 2 + d * tn * 2 + tm * tn * osize) + 5 * tm * tn * 4
              + sum(2 * 4 * shape[0] * shape[1] for _, shape, _ in extras))
    return pl.pallas_call(
        functools.partial(_proj_deferred_kernel, epilogue=epilogue, ni=ni),
        grid=(width // tn, ni + 1),
        in_specs=in_specs,
        out_specs=pl.BlockSpec((tm, tn), lambda j, i: (epi_tile(i), j)),
        out_shape=jax.ShapeDtypeStruct((r, width), out_dtype),
        scratch_shapes=[pltpu.VMEM((tm, tn), F32)],
        compiler_params=_params(("parallel", "arbitrary"), nbytes),
        name="proj_deferred",
    )(n, w, *[a for a, _, _ in extras])


def _retention_kernel(lg_ref, q_ref, k_ref, v_ref, g_ref, o_ref, bw_ref, *, nc):
    c_len = RET_CHUNK
    h = pl.program_id(1)
    lgf = lg_ref[0, h]
    lgb = lg_ref[1, h]
    ii = lax.broadcasted_iota(jnp.int32, (c_len, c_len), 0).astype(F32)
    jj = lax.broadcasted_iota(jnp.int32, (c_len, c_len), 1).astype(F32)
    diff = ii - jj
    decay = jnp.where(diff >= 0, jnp.exp(jnp.maximum(diff, 0.0) * lgf),
                      jnp.exp(jnp.maximum(-diff, 0.0) * lgb))
    ci = lax.broadcasted_iota(jnp.int32, (c_len, 1), 0).astype(F32)
    xi_f = jnp.exp((ci + 1.0) * lgf)
    zeta_f = jnp.exp((c_len - 1.0 - ci) * lgf)
    xi_b = jnp.exp((c_len - ci) * lgb)
    zeta_b = jnp.exp(ci * lgb)
    one = jnp.ones((1, 1), F32)
    chunk_f = jnp.exp(one * (c_len * lgf))
    chunk_b = jnp.exp(one * (c_len * lgb))
    tn_dims = (((0,), (0,)), ((), ()))
    nt_dims = (((1,), (1,)), ((), ()))

    def rows_of(c):
        return pl.ds(pl.multiple_of(c * c_len, c_len), c_len)

    group = RET_UNROLL if nc % RET_UNROLL == 0 else 1

    def state_chain(state, kvs, chunk_decay):
        before = []
        for kv in kvs:
            before.append(state)
            state = state * chunk_decay + kv
        return before, state

    def bw_body(t, state):
        rows = [rows_of(nc - 1 - (t * group + u)) for u in range(group)]
        kvs = [lax.dot_general((k_ref[r, :].astype(F32) * zeta_b).astype(BF16), v_ref[r, :], tn_dims,
                               preferred_element_type=F32) for r in rows]
        before, state = state_chain(state, kvs, chunk_b)
        for r, s_in in zip(rows, before):
            qx = (q_ref[r, :].astype(F32) * xi_b).astype(BF16)
            bw_ref[r, :] = jnp.dot(qx, s_in.astype(BF16), preferred_element_type=F32)
        return state

    lax.fori_loop(0, nc // group, bw_body, jnp.zeros((HEAD_DIM, HEAD_DIM), F32))

    def fw_body(t, state):
        rows = [rows_of(t * group + u) for u in range(group)]
        scores = [lax.dot_general(q_ref[r, :], k_ref[r, :], nt_dims, preferred_element_type=F32)
                  for r in rows]
        kvs = [lax.dot_general((k_ref[r, :].astype(F32) * zeta_f).astype(BF16), v_ref[r, :], tn_dims,
                               preferred_element_type=F32) for r in rows]
        before, state = state_chain(state, kvs, chunk_f)
        inters = [jnp.dot((q_ref[r, :].astype(F32) * xi_f).astype(BF16), s_in.astype(BF16),
                          preferred_element_type=F32) for r, s_in in zip(rows, before)]
        for r, s, inter in zip(rows, scores, inters):
            intra = jnp.dot((s * decay).astype(BF16), v_ref[r, :], preferred_element_type=F32)
            y = (intra + inter) + bw_ref[r, :]
            y = y * lax.rsqrt(jnp.mean(y * y, axis=-1, keepdims=True) + EPS)
            o_ref[r, :] = (g_ref[r, :] * y).astype(o_ref.dtype)
        return state

    lax.fori_loop(0, nc // group, fw_body, jnp.zeros((HEAD_DIM, HEAD_DIM), F32))


def _retention(qk, v, gate, log_g, row0, batch, seq):
    assert row0 % seq == 0 and seq % RET_CHUNK == 0
    sb0 = row0 // seq
    hd = HEAD_DIM
    nbytes = 2 * seq * hd * (2 + 2 + 2 + 4 + 2) + seq * hd * 4 + 64 * hd * hd * 4
    return pl.pallas_call(
        functools.partial(_retention_kernel, nc=seq // RET_CHUNK),
        grid=(batch, RET_HEADS),
        in_specs=[pl.BlockSpec(memory_space=pltpu.SMEM),
                  pl.BlockSpec((seq, hd), lambda b, h: (sb0 + b, h)),
                  pl.BlockSpec((seq, hd), lambda b, h: (sb0 + b, RET_HEADS + h)),
                  pl.BlockSpec((seq, hd), lambda b, h: (sb0 + b, h)),
                  pl.BlockSpec((seq, hd), lambda b, h: (sb0 + b, h))],
        out_specs=pl.BlockSpec((seq, hd), lambda b, h: (b, h)),
        out_shape=jax.ShapeDtypeStruct((batch * seq, RET_HEADS * hd), BF16),
        scratch_shapes=[pltpu.VMEM((seq, hd), F32)],
        compiler_params=_params(("parallel", "parallel"), nbytes),
        name="retention",
    )(log_g, qk, qk, v, gate)


def _gqa_kernel(q_ref, k_ref, vt_ref, o_ref, qs_ref, m_ref, l_ref, acc_ref, st_ref, *, nk, tq, group, cb):
    hd = HEAD_DIM
    r = group * tq
    for g in range(group):
        qs_ref[g * tq:(g + 1) * tq, :] = q_ref[:, g * hd:(g + 1) * hd]
    m_ref[...] = jnp.full(m_ref.shape, -jnp.inf, F32)
    l_ref[...] = jnp.zeros(l_ref.shape, F32)
    acc_ref[...] = jnp.zeros(acc_ref.shape, F32)
    nt_dims = (((1,), (1,)), ((), ()))

    nb = r // cb

    def scores(c, j):
        return lax.dot_general(k_ref[c], qs_ref[j * cb:(j + 1) * cb, :], nt_dims,
                               preferred_element_type=F32)

    st_ref[...] = scores(0, 0)

    def body(c, carry):
        vt = vt_ref[c]
        st = st_ref[...]
        for j in range(nb):
            cols = slice(j * cb, (j + 1) * cb)
            if j + 1 < nb:
                st_next = scores(c, j + 1)
            else:
                st_next = scores(jnp.minimum(c + 1, nk - 1), 0)
            m_prev = m_ref[:, cols]
            m_new = jnp.maximum(m_prev, jnp.max(st, axis=0, keepdims=True))
            alpha = jnp.exp2(m_prev - m_new)
            p = jnp.exp2(st - m_new)
            l_ref[:, cols] = alpha * l_ref[:, cols] + jnp.sum(p, axis=0, keepdims=True)
            acc_ref[:, cols] = alpha * acc_ref[:, cols] + jnp.dot(vt, p.astype(BF16),
                                                                  preferred_element_type=F32)
            m_ref[:, cols] = m_new
            st = st_next
        st_ref[...] = st
        return carry

    lax.fori_loop(0, nk, body, 0)
    out = (acc_ref[...] / l_ref[...]).T
    for g in range(group):
        o_ref[:, g * hd:(g + 1) * hd] = out[g * tq:(g + 1) * tq, :].astype(o_ref.dtype)


def _gqa(aq, ak, avt, row0, batch, seq):
    hd = HEAD_DIM
    group = ATT_Q_HEADS // ATT_KV_HEADS
    tq = _tile(seq, GQA_Q_ROWS)
    tk = avt.shape[2]
    assert row0 % seq == 0 and seq % tk == 0
    sb0 = row0 // seq
    qb0 = row0 // tq
    nq = seq // tq
    nk = seq // tk
    gw = group * hd
    r = group * tq
    cb = _tile(r, GQA_COL_BLOCK)
    ak3 = ak.reshape(ak.shape[0] // tk, tk, ak.shape[1])
    nbytes = (2 * (2 * tq * gw * 2 + 2 * seq * hd * 2) + r * hd * (2 + 4) + 2 * 8 * r * 4
              + 6 * tk * cb * 4 + 2 * r * hd * 4)
    return pl.pallas_call(
        functools.partial(_gqa_kernel, nk=nk, tq=tq, group=group, cb=cb),
        grid=(batch, ATT_KV_HEADS, nq),
        in_specs=[pl.BlockSpec((tq, gw), lambda b, h, i: (qb0 + b * nq + i, h)),
                  pl.BlockSpec((nk, tk, hd), lambda b, h, i: (sb0 + b, 0, h)),
                  pl.BlockSpec((nk, hd, tk), lambda b, h, i: (sb0 + b, h, 0))],
        out_specs=pl.BlockSpec((tq, gw), lambda b, h, i: (b * nq + i, h)),
        out_shape=jax.ShapeDtypeStruct((batch * seq, ATT_Q_HEADS * hd), BF16),
        scratch_shapes=[pltpu.VMEM((r, hd), BF16),
                        pltpu.VMEM((1, r), F32),
                        pltpu.VMEM((1, r), F32),
                        pltpu.VMEM((hd, r), F32),
                        pltpu.VMEM((tk, cb), F32)],
        compiler_params=_params(("parallel", "parallel", "parallel"), nbytes),
        name="gqa",
    )(aq, ak3, avt)


def _part_specs(parts, tm, cols, row_axis, col_block=None):
    na = parts[0].shape[0] // tm
    assert parts[0].shape[0] % tm == 0 and parts[1].shape[0] % tm == 0

    def spec(first):
        def index_map(*idx):
            i = idx[row_axis]
            row = jnp.minimum(i, na - 1) if first else jnp.maximum(i - na, 0)
            return (row, 0 if col_block is None else col_block(*idx))
        return pl.BlockSpec((tm, cols), index_map)

    return [spec(True), spec(False)], na


def _select_part(i, na, a_ref, b_ref):
    return jnp.where(i < na, a_ref[...], b_ref[...])


def _out_proj_kernel(ra_ref, rb_ref, aa_ref, ab_ref, w_ref, *rest, rw, na):
    i = pl.program_id(1)
    ret = _select_part(i, na, ra_ref, rb_ref)
    att = _select_part(i, na, aa_ref, ab_ref)
    acc = jnp.dot(ret, w_ref[:rw, :], preferred_element_type=F32)
    acc = acc + jnp.dot(att, w_ref[rw:, :], preferred_element_type=F32)
    if len(rest) == 2:
        x_ref, o_ref = rest
        acc = x_ref[...] + acc
    else:
        o_ref, = rest
    o_ref[...] = acc


def _out_proj(ret_parts, att_parts, w, layer, x=None):
    t = ret_parts[0].shape[0] + ret_parts[1].shape[0]
    d = w.shape[2]
    rw, aw = ret_parts[0].shape[1], att_parts[0].shape[1]
    tm = _tile(min(ret_parts[0].shape[0], ret_parts[1].shape[0]), OUT_PROJ_ROWS)
    tn = _tile(d, COL_TILE)
    ret_specs, na = _part_specs(ret_parts, tm, rw, 1)
    att_specs, _ = _part_specs(att_parts, tm, aw, 1)
    in_specs = ret_specs + att_specs + [pl.BlockSpec((None, rw + aw, tn), lambda j, i: (layer, 0, j))]
    operands = [*ret_parts, *att_parts, w]
    if x is not None:
        in_specs.append(pl.BlockSpec((tm, tn), lambda j, i: (i, j)))
        operands.append(x)
    nbytes = (2 * (2 * tm * (rw + aw) * 2 + (rw + aw) * tn * 2 + 2 * tm * tn * 4) + 3 * tm * tn * 4
              + 2 * tm * (rw + aw) * 2)
    return pl.pallas_call(
        functools.partial(_out_proj_kernel, rw=rw, na=na),
        grid=(d // tn, t // tm),
        in_specs=in_specs,
        out_specs=pl.BlockSpec((tm, tn), lambda j, i: (i, j)),
        out_shape=jax.ShapeDtypeStruct((t, d), F32),
        compiler_params=_params(("parallel", "parallel"), nbytes),
        name="out_proj",
    )(*operands)


def _cross_kernel(*refs, na):
    *x_refs, kv_ref, wq_ref, wo_ref, gc_ref, gf_ref, x_out_ref, n_out_ref = refs
    hd = MEM_HEAD_DIM
    mw = MEM_HEADS * hd
    if len(x_refs) == 1:
        x = x_refs[0][...]
    else:
        xa_ref, xb_ref, mix_ref = x_refs
        x = _select_part(pl.program_id(0), na, xa_ref, xb_ref) + mix_ref[...]
    n = _rms(x, gc_ref[...]).astype(BF16)
    q = jnp.dot(n, wq_ref[...], preferred_element_type=F32).astype(BF16)
    scale = hd ** -0.5
    nt_dims = (((1,), (1,)), ((), ()))
    heads = []
    for h in range(MEM_HEADS):
        k = kv_ref[:, h * hd:(h + 1) * hd]
        v = kv_ref[:, mw + h * hd:mw + (h + 1) * hd]
        s = lax.dot_general(q[:, h * hd:(h + 1) * hd], k, nt_dims, preferred_element_type=F32) * scale
        p = jnp.exp(s - jnp.max(s, axis=-1, keepdims=True))
        p = p / jnp.sum(p, axis=-1, keepdims=True)
        heads.append(jnp.dot(p.astype(BF16), v, preferred_element_type=F32).astype(BF16))
    o = jnp.concatenate(heads, axis=-1)
    x2 = x + jnp.dot(o, wo_ref[...], preferred_element_type=F32)
    x_out_ref[...] = x2
    n_out_ref[...] = _rms(x2, gf_ref[...]).astype(n_out_ref.dtype)


def _cross(x, kv, wq, wo, layer, g_cross, g_ffn, mem_block_of_tile, tm, x_parts=None):
    t, d = x.shape
    m = kv.shape[0] // mem_block_of_tile.n_seq
    mw = wq.shape[2]
    x_specs, x_operands, na = [pl.BlockSpec((tm, d), lambda i: (i, 0))], [x], 0
    if x_parts is not None:
        part_specs, na = _part_specs(x_parts, tm, d, 0)
        x_specs, x_operands = part_specs + x_specs, [*x_parts, x]
    nbytes = (2 * (tm * d * (4 * len(x_operands) + 4 + 2) + m * 2 * mw * 2 + 2 * d * mw * 2 + 2 * d * 4)
              + 6 * tm * d * 4)
    return pl.pallas_call(
        functools.partial(_cross_kernel, na=na),
        grid=(t // tm,),
        in_specs=[*x_specs,
                  pl.BlockSpec((m, 2 * mw), lambda i: (mem_block_of_tile(i), 0)),
                  pl.BlockSpec((None, d, mw), lambda i: (layer, 0, 0)),
                  pl.BlockSpec((None, mw, d), lambda i: (layer, 0, 0)),
                  pl.BlockSpec((1, d), lambda i: (0, 0)),
                  pl.BlockSpec((1, d), lambda i: (0, 0))],
        out_specs=[pl.BlockSpec((tm, d), lambda i: (i, 0)),
                   pl.BlockSpec((tm, d), lambda i: (i, 0))],
        out_shape=[jax.ShapeDtypeStruct((t, d), F32), jax.ShapeDtypeStruct((t, d), BF16)],
        compiler_params=_params(("parallel",), nbytes),
        name="cross",
    )(*x_operands, kv, wq, wo, g_cross.reshape(1, d), g_ffn.reshape(1, d))


class _SeqLayout:
    def __init__(self, n_a, len_a, n_b, len_b):
        self.n_a, self.len_a, self.n_b, self.len_b = n_a, len_a, n_b, len_b
        self.rows_a = n_a * len_a
        self.n_seq = n_a + n_b

    def seq_of_row(self, row):
        return jnp.where(row < self.rows_a, row // self.len_a,
                         self.n_a + (row - self.rows_a) // self.len_b)

    def pos_in_seq(self, row):
        return jnp.where(row < self.rows_a, row % self.len_a, (row - self.rows_a) % self.len_b)

    def seq_len_at(self, row):
        return jnp.where(row < self.rows_a, self.len_a, self.len_b)


class _MemBlockOfTile:
    def __init__(self, layout, tm):
        self.layout, self.tm, self.n_seq = layout, tm, layout.n_seq

    def __call__(self, i):
        return self.layout.seq_of_row(i * self.tm)


def _gelu(a):
    return 0.5 * a * (1.0 + lax.erf(a * (2.0 ** -0.5)))


def _ffn_kernel(n_hbm, wa_ref, wu_ref, cwa_ref, cwu_ref, cba_ref, cbu_ref, wd_ref, x_hbm, g_ref,
                out0_hbm, out1_hbm, lhs_ref, h_ref, act_ref, acc_ref, sems,
                *, layout, tm, tf, nf, final, up_chunks):
    outs = (out0_hbm, out1_hbm)
    i = pl.program_id(0)
    f = pl.program_id(1)
    row0 = pl.multiple_of(i * tm, tm)
    first_down = 2
    last_step = nf + 1

    def residual_copy():
        return pltpu.make_async_copy(x_hbm.at[pl.ds(row0, tm), :], acc_ref, sems.at[0])

    def rows_copy(src_row, dst_row, rows, sem):
        return pltpu.make_async_copy(n_hbm.at[pl.ds(src_row, rows), :],
                                     lhs_ref.at[pl.ds(dst_row, rows), :], sems.at[sem])

    slabs = tf // LANES

    def conv(slab, cw, cb):
        lo = h_ref[slab, pl.ds(HALO - 1, tm), :]
        mid = h_ref[slab, pl.ds(HALO, tm), :]
        hi = h_ref[slab, pl.ds(HALO + 1, tm), :]
        return lo * cw[0:1, :] + mid * cw[1:2, :] + hi * cw[2:3, :] + cb

    def gate_into_act():
        for s in range(slabs):
            lanes = slice(s * LANES, (s + 1) * LANES)
            a = conv(s, cwa_ref[:, lanes], cba_ref[:, lanes])
            u = conv(slabs + s, cwu_ref[:, lanes], cbu_ref[:, lanes])
            act_ref[:, lanes] = (_gelu(a) * u).astype(BF16)

    def up_project():
        for r0, r1 in up_chunks:
            lhs = lhs_ref[r0:r1, :]
            ha = jnp.dot(lhs, wa_ref[...], preferred_element_type=F32)
            hu = jnp.dot(lhs, wu_ref[...], preferred_element_type=F32)
            for s in range(slabs):
                lanes = slice(s * LANES, (s + 1) * LANES)
                h_ref[s, r0:r1, :] = ha[:, lanes]
                h_ref[slabs + s, r0:r1, :] = hu[:, lanes]

    @pl.when(f == 0)
    def _():
        residual_copy().start()
        pos = layout.pos_in_seq(row0)
        first = pos == 0
        last = pos + tm == layout.seq_len_at(row0)
        main = rows_copy(row0, HALO, tm, 1)
        main.start()
        zero = jnp.zeros((HALO, lhs_ref.shape[1]), lhs_ref.dtype)

        @pl.when(first)
        def _():
            lhs_ref[0:HALO, :] = zero

        @pl.when(jnp.logical_not(first))
        def _():
            prev = rows_copy(row0 - HALO, 0, HALO, 2)
            prev.start()
            prev.wait()

        @pl.when(last)
        def _():
            lhs_ref[HALO + tm:, :] = zero

        @pl.when(jnp.logical_not(last))
        def _():
            nxt = rows_copy(row0 + tm, HALO + tm, HALO, 3)
            nxt.start()
            nxt.wait()

        main.wait()

    @pl.when(f == first_down)
    def _():
        residual_copy().wait()

    def stages(down, gate, up):
        if down:
            acc_ref[...] += jnp.dot(act_ref[...], wd_ref[...], preferred_element_type=F32)
        if gate:
            gate_into_act()
        if up:
            up_project()

    pl.when(f == 0)(functools.partial(stages, False, False, True))
    pl.when(f == 1)(functools.partial(stages, False, True, nf > 1))
    if nf > 2:
        pl.when(jnp.logical_and(f >= 2, f < nf))(functools.partial(stages, True, True, True))
    if nf > 1:
        pl.when(f == nf)(functools.partial(stages, True, True, False))
    pl.when(f == last_step)(functools.partial(stages, True, False, False))

    def normalize_rows(store):
        def body(c, carry):
            r0 = pl.multiple_of(c * NORM_CHUNK, NORM_CHUNK)
            store(r0, _rms(acc_ref[pl.ds(r0, NORM_CHUNK), :], g_ref[...]))
            return carry

        lax.fori_loop(0, tm // NORM_CHUNK, body, 0)

    @pl.when(f == last_step)
    def _():
        if final:
            ya_hbm, yb_hbm = outs

            def store(r0, y):
                acc_ref[pl.ds(r0, NORM_CHUNK), :] = y

            normalize_rows(store)

            @pl.when(row0 < layout.rows_a)
            def _():
                out = pltpu.make_async_copy(acc_ref, ya_hbm.at[pl.ds(row0, tm), :], sems.at[4])
                out.start()
                out.wait()

            @pl.when(row0 >= layout.rows_a)
            def _():
                out = pltpu.make_async_copy(acc_ref, yb_hbm.at[pl.ds(row0 - layout.rows_a, tm), :],
                                            sems.at[4])
                out.start()
                out.wait()
        else:
            x_hbm_out, n_hbm_out = outs
            out_x = pltpu.make_async_copy(acc_ref, x_hbm_out.at[pl.ds(row0, tm), :], sems.at[4])
            out_x.start()

            def store(r0, y):
                lhs_ref[pl.ds(pl.multiple_of(HALO + r0, HALO), NORM_CHUNK), :] = y.astype(BF16)

            normalize_rows(store)
            out_n = pltpu.make_async_copy(lhs_ref.at[pl.ds(HALO, tm), :],
                                          n_hbm_out.at[pl.ds(row0, tm), :], sems.at[5])
            out_n.start()
            out_x.wait()
            out_n.wait()


def _ffn(n, x, w_up, conv_w, conv_b, w_down, layer, layout, gain, final):
    t, d = x.shape
    ff = w_down.shape[1]
    tm = _tile(min(layout.len_a, layout.len_b), FFN_ROWS)
    tf = _tile(ff, FFN_COLS)
    assert layout.len_a % tm == 0 and layout.len_b % tm == 0 and tm % HALO == 0
    nf = ff // tf
    conv_b = conv_b.reshape(1, 2 * ff)

    def up_tile(f):
        return jnp.minimum(f, nf - 1)

    def gate_tile(f):
        return jnp.clip(f - 1, 0, nf - 1)

    def down_tile(f):
        return jnp.clip(f - 2, 0, nf - 1)

    hidden_rows = tm + 2 * HALO
    n_chunks = min(FFN_UP_CHUNKS, hidden_rows // HALO)
    bounds = [HALO * ((hidden_rows // HALO) * c // n_chunks) for c in range(n_chunks + 1)]
    up_chunks = tuple(zip(bounds[:-1], bounds[1:]))
    nbytes =(2 * (2 * d * tf * 2 + tf * d * 2) + hidden_rows * (d * 2 + 2 * tf * 4)
              + tm * d * 4 + 4 * hidden_rows * tf * 4 + 3 * tm * tf * 4)
    assert tm % NORM_CHUNK == 0
    if final:
        rows_b = t - layout.rows_a
        out_shape = [jax.ShapeDtypeStruct((layout.rows_a, d), F32), jax.ShapeDtypeStruct((rows_b, d), F32)]
    else:
        out_shape = [jax.ShapeDtypeStruct((t, d), F32), jax.ShapeDtypeStruct((t, d), BF16)]
    return pl.pallas_call(
        functools.partial(_ffn_kernel, layout=layout, tm=tm, tf=tf, nf=nf, final=final,
                          up_chunks=up_chunks),
        grid=(t // tm, nf + 2),
        in_specs=[pl.BlockSpec(memory_space=pl.ANY),
                  pl.BlockSpec((None, d, tf), lambda i, f: (layer, 0, up_tile(f))),
                  pl.BlockSpec((None, d, tf), lambda i, f: (layer, 0, nf + up_tile(f))),
                  pl.BlockSpec((3, tf), lambda i, f: (0, gate_tile(f))),
                  pl.BlockSpec((3, tf), lambda i, f: (0, nf + gate_tile(f))),
                  pl.BlockSpec((1, tf), lambda i, f: (0, gate_tile(f))),
                  pl.BlockSpec((1, tf), lambda i, f: (0, nf + gate_tile(f))),
                  pl.BlockSpec((None, tf, d), lambda i, f: (layer, down_tile(f), 0)),
                  pl.BlockSpec(memory_space=pl.ANY),
                  pl.BlockSpec((1, d), lambda i, f: (0, 0))],
        out_specs=[pl.BlockSpec(memory_space=pl.ANY), pl.BlockSpec(memory_space=pl.ANY)],
        out_shape=out_shape,
        scratch_shapes=[pltpu.VMEM((hidden_rows, d), BF16),
                        pltpu.VMEM((2 * tf // LANES, hidden_rows, LANES), F32),
                        pltpu.VMEM((tm, tf), BF16),
                        pltpu.VMEM((tm, d), F32),
                        pltpu.SemaphoreType.DMA((6,))],
        compiler_params=_params(("arbitrary", "arbitrary"), nbytes),
        name="ffn",
    )(n, w_up, w_up, conv_w, conv_w, conv_b, conv_b, w_down, x, gain.reshape(1, d))


def _rope_tables(layout):
    pos = jnp.concatenate([jnp.tile(jnp.arange(layout.len_a), layout.n_a),
                           jnp.tile(jnp.arange(layout.len_b), layout.n_b)])
    r = (pos // GRID_W).astype(F32)
    c = (pos % GRID_W).astype(F32)
    half = HEAD_DIM // 2
    freqs = ROPE_THETA ** (-jnp.arange(0, half, 2, dtype=F32) / half)
    ang_r = r[:, None] * freqs
    ang_c = c[:, None] * freqs
    cos = jnp.concatenate([jnp.cos(ang_r)] * 2 + [jnp.cos(ang_c)] * 2, axis=-1)
    sin = jnp.concatenate([-jnp.sin(ang_r), jnp.sin(ang_r), -jnp.sin(ang_c), jnp.sin(ang_c)], axis=-1)
    return cos, sin


def kernel(x_prompt, x_sample, mem_prompt, mem_sample, w_in, w_out, ret_decay_logit, q_norm, k_norm, norm_mix, norm_cross, norm_mem, w_cq, w_ckv, w_co, norm_ffn, w_up, conv_w, conv_b, w_down, norm_final):
    b, s, d = x_prompt.shape
    db, ds, _ = x_sample.shape
    m = mem_prompt.shape[1]
    depth = w_in.shape[0]
    layout = _SeqLayout(b, s, db, ds)
    t1, t2 = b * s, db * ds
    t = t1 + t2
    rw = RET_HEADS * HEAD_DIM
    aqw = ATT_Q_HEADS * HEAD_DIM
    akw = ATT_KV_HEADS * HEAD_DIM
    assert w_in.shape[2] == 4 * rw + aqw + 2 * akw and rw + aqw == w_out.shape[1]

    x_groups = (x_prompt.reshape(t1, d), x_sample.reshape(t2, d))
    mem =jnp.concatenate([mem_prompt.reshape(b * m, d), mem_sample.reshape(db * m, d)], axis=0)
    cos, sin = _rope_tables(layout)
    log_g = jax.nn.log_sigmoid(ret_decay_logit.astype(F32))

    tm = _tile(t, ROW_TILE)
    tn = _tile(rw, COL_TILE)
    rope_extras = [(cos, (tm, HEAD_DIM), lambda j, i: (i, 0)),
                   (sin, (tm, HEAD_DIM), lambda j, i: (i, 0))]
    gqa_tk = _tile(min(s, ds), GQA_KV_ROWS)
    cross_tm = _tile(min(s, ds), CROSS_ROWS)
    mem_block = _MemBlockOfTile(layout, cross_tm)

    wi, wo = w_in.astype(BF16), w_out.astype(BF16)
    wcq, wckv, wco = w_cq.astype(BF16), w_ckv.astype(BF16), w_co.astype(BF16)
    wup, wdn = w_up.astype(BF16), w_down.astype(BF16)

    n = _rmsnorm_parts(x_groups, norm_mix[0], BF16)
    x = None
    for l in range(depth):
        qk = _proj_deferred(n, wi, l, 0, 2 * rw, tn, BF16,
                            functools.partial(_epi_ret_qk, k_block0=rw // tn, k_scale=HEAD_DIM ** -0.5),
                            rope_extras)
        rv = _proj(n, wi, l, 2 * rw, rw, tn, BF16, _epi_plain)
        gate = _proj(n, wi, l, 3 * rw, rw, tn, F32, _epi_silu)
        aq = _proj_deferred(n, wi, l, 4 * rw, aqw, _tile(aqw, COL_TILE), BF16, _epi_att_q,
                            rope_extras + [(q_norm[l].reshape(1, HEAD_DIM), (1, HEAD_DIM),
                                            lambda j, i: (0, 0))])
        ak = _proj_deferred(n, wi, l, 4 * rw + aqw, akw, akw, BF16, _epi_att_k,
                            rope_extras + [(k_norm[l].reshape(1, HEAD_DIM), (1, HEAD_DIM),
                                            lambda j, i: (0, 0))])
        avt = _proj(n, wi, l, 4 * rw + aqw + akw, akw, akw, BF16, _epi_transposed_chunks,
                    transposed_chunk=gqa_tk)
        ret = (_retention(qk, rv, gate, log_g[l], 0, b, s), _retention(qk, rv, gate, log_g[l], t1, db, ds))
        att = (_gqa(aq, ak, avt, 0, b, s), _gqa(aq, ak, avt, t1, db, ds))
        x = _out_proj(ret, att, wo, l, x)

        memn = _rmsnorm(mem, norm_mem[l], BF16)
        kv = _proj(memn, wckv, l, 0, w_ckv.shape[2], _tile(w_ckv.shape[2], COL_TILE),
                   BF16, _epi_plain, tm_pref=NORM_ROWS)
        x, n = _cross(x, kv, wcq, wco, l, norm_cross[l], norm_ffn[l], mem_block, cross_tm,
                      x_parts=x_groups if l == 0 else None)
        if l + 1 < depth:
            x, n = _ffn(n, x, wup, conv_w[l], conv_b[l], wdn, l, layout, norm_mix[l + 1], final=False)
        else:
            y_a, y_b = _ffn(n, x, wup, conv_w[l], conv_b[l], wdn, l, layout, norm_final, final=True)

    return (y_a.reshape(b, s, d), y_b.reshape(db, ds, d))
```

```python
import functools

import jax
import jax.numpy as jnp
from jax import lax
from jax.experimental import pallas as pl
from jax.experimental.pallas import tpu as pltpu

F32 = jnp.float32
BF16 = jnp.bfloat16

HEAD_DIM = 128
RET_HEADS = 16
ATT_Q_HEADS = 16
ATT_KV_HEADS = 4
RET_CHUNK = 128
GRID_W = 64
ROPE_THETA = 10000.0
MEM_HEADS = 4
MEM_HEAD_DIM = 128
EPS = 1e-6
LOG2E = 1.4426950408889634

V7X_VMEM_BYTES = 64 * 1024 * 1024
V7X_VMEM_CAP = 60000 * 1024
LANES = 128
BF16_SUBLANES = 16

ROW_TILE = 1024
COL_TILE = 1024
NORM_ROWS = 256
CROSS_ROWS = 256
OUT_PROJ_ROWS = 512
GQA_Q_ROWS = 512
GQA_KV_ROWS = 1024
GQA_COL_BLOCK = 512
RET_UNROLL = 16
FFN_ROWS = 1024
FFN_COLS = 256
FFN_UP_CHUNKS = 8
HALO = BF16_SUBLANES
NORM_CHUNK = 64


def _tile(n, pref):
    if n <= pref:
        return n
    t = pref
    while n % t:
        t //= 2
    return t


def _params(sem, vmem_bytes):
    limit = min(int(vmem_bytes), V7X_VMEM_CAP)
    return pltpu.CompilerParams(dimension_semantics=sem, vmem_limit_bytes=limit)


def _rms(x, g):
    y = x * lax.rsqrt(jnp.mean(x * x, axis=-1, keepdims=True) + EPS)
    return y * g


def _rmsnorm_kernel(x_ref, g_ref, o_ref):
    o_ref[...] = _rms(x_ref[...], g_ref[...]).astype(o_ref.dtype)


def _rmsnorm(x, g, out_dtype, row0=0, rows=None):
    d = x.shape[1]
    rows = x.shape[0] if rows is None else rows
    tm = _tile(rows, NORM_ROWS)
    assert row0 % tm == 0
    blk0 = row0 // tm
    nbytes = 2 * tm * d * (4 + jnp.dtype(out_dtype).itemsize) + 3 * tm * d * 4
    return pl.pallas_call(
        _rmsnorm_kernel,
        grid=(rows // tm,),
        in_specs=[pl.BlockSpec((tm, d), lambda i: (blk0 + i, 0)),
                  pl.BlockSpec((1, d), lambda i: (0, 0))],
        out_specs=pl.BlockSpec((tm, d), lambda i: (i, 0)),
        out_shape=jax.ShapeDtypeStruct((rows, d), out_dtype),
        compiler_params=_params(("parallel",), nbytes),
        name="rmsnorm",
    )(x, g.reshape(1, d))


def _rmsnorm_parts_kernel(a_ref, b_ref, g_ref, o_ref, *, na):
    x = _select_part(pl.program_id(0), na, a_ref, b_ref)
    o_ref[...] = _rms(x, g_ref[...]).astype(o_ref.dtype)


def _rmsnorm_parts(parts, g, out_dtype):
    d = parts[0].shape[1]
    t = parts[0].shape[0] + parts[1].shape[0]
    tm = _tile(min(parts[0].shape[0], parts[1].shape[0]), NORM_ROWS)
    specs, na = _part_specs(parts, tm, d, 0)
    nbytes = 2 * tm * d * (8 + jnp.dtype(out_dtype).itemsize) + 4 * tm * d * 4
    return pl.pallas_call(
        functools.partial(_rmsnorm_parts_kernel, na=na),
        grid=(t // tm,),
        in_specs=specs + [pl.BlockSpec((1, d), lambda i: (0, 0))],
        out_specs=pl.BlockSpec((tm, d), lambda i: (i, 0)),
        out_shape=jax.ShapeDtypeStruct((t, d), out_dtype),
        compiler_params=_params(("parallel",), nbytes),
        name="rmsnorm_parts",
    )(*parts, g.reshape(1, d))


def _rope(a, cos, sin, low_half):
    partner = jnp.where(low_half, pltpu.roll(a, 96, 1), pltpu.roll(a, 32, 1))
    return a * cos + partner * sin


def _low_half_mask(rows):
    lane = lax.broadcasted_iota(jnp.int32, (rows, HEAD_DIM), 1)
    return (lane & 32) == 0


def _proj_kernel(n_ref, w_ref, *rest, epilogue):
    acc = jnp.dot(n_ref[...], w_ref[...], preferred_element_type=F32)
    epilogue(acc, *rest)


def _epi_plain(acc, o_ref):
    o_ref[...] = acc.astype(o_ref.dtype)


def _epi_silu(acc, o_ref):
    o_ref[...] = jax.nn.silu(acc).astype(o_ref.dtype)


def _epi_ret_qk(acc, cos_ref, sin_ref, o_ref, *, k_block0, k_scale):
    j = pl.program_id(0)
    scale = jnp.where(j >= k_block0, k_scale, 1.0).astype(F32)
    cos, sin = cos_ref[...], sin_ref[...]
    low = _low_half_mask(acc.shape[0])
    for h in range(acc.shape[1] // HEAD_DIM):
        sl = slice(h * HEAD_DIM, (h + 1) * HEAD_DIM)
        o_ref[:, sl] = (_rope(acc[:, sl], cos, sin, low) * scale).astype(o_ref.dtype)


def _norm_rope_heads(acc, cos, sin, g, o_ref, scale=None):
    low = _low_half_mask(acc.shape[0])
    for h in range(acc.shape[1] // HEAD_DIM):
        sl = slice(h * HEAD_DIM, (h + 1) * HEAD_DIM)
        y = _rope(_rms(acc[:, sl], g), cos, sin, low)
        o_ref[:, sl] = (y if scale is None else y * scale).astype(o_ref.dtype)


def _epi_att_q(acc, cos_ref, sin_ref, g_ref, o_ref):
    _norm_rope_heads(acc, cos_ref[...], sin_ref[...], g_ref[...], o_ref, scale=HEAD_DIM ** -0.5 * LOG2E)


def _epi_att_k(acc, cos_ref, sin_ref, g_ref, o_ref):
    _norm_rope_heads(acc, cos_ref[...], sin_ref[...], g_ref[...], o_ref)


def _epi_transposed_chunks(acc, o_ref):
    tk = o_ref.shape[2]
    for c in range(o_ref.shape[0]):
        o_ref[c] = acc[c * tk:(c + 1) * tk, :].T.astype(o_ref.dtype)


def _proj(n, w, layer, col0, width, tn, out_dtype, epilogue, extras=(), tm_pref=None,
          transposed_chunk=None):
    r, d = n.shape
    tm = _tile(r, ROW_TILE if tm_pref is None else tm_pref)
    assert col0 % tn == 0 and width % tn == 0
    cb0 = col0 // tn
    in_specs = [pl.BlockSpec((tm, d), lambda j, i: (i, 0)),
                pl.BlockSpec((None, d, tn), lambda j, i: (layer, 0, cb0 + j))]
    in_specs += [pl.BlockSpec(shape, imap) for _, shape, imap in extras]
    osize = jnp.dtype(out_dtype).itemsize
    nbytes = (2 * (tm * d * 2 + d * tn * 2 + tm * tn * osize) + 4 * tm * tn * 4
              + sum(2 * 4 * shape[0] * shape[1] for _, shape, _ in extras))
    if transposed_chunk is None:
        out_spec = pl.BlockSpec((tm, tn), lambda j, i: (i, j))
        out_shape = jax.ShapeDtypeStruct((r, width), out_dtype)
    else:
        tk = transposed_chunk
        assert tm % tk == 0
        out_spec = pl.BlockSpec((tm // tk, tn, tk), lambda j, i: (i, j, 0))
        out_shape = jax.ShapeDtypeStruct((r // tk, width, tk), out_dtype)
    return pl.pallas_call(
        functools.partial(_proj_kernel, epilogue=epilogue),
        grid=(width // tn, r // tm),
        in_specs=in_specs,
        out_specs=out_spec,
        out_shape=out_shape,
        compiler_params=_params(("parallel", "parallel"), nbytes),
        name="proj",
    )(n, w, *[a for a, _, _ in extras])


def _proj_deferred_kernel(n_ref, w_ref, *rest, epilogue, ni):
    *epi_refs, prev_ref = rest
    i = pl.program_id(1)

    def step(multiply, finish):
        if multiply:
            acc = jnp.dot(n_ref[...], w_ref[...], preferred_element_type=F32)
        if finish:
            epilogue(prev_ref, *epi_refs)
        if multiply:
            prev_ref[...] = acc

    pl.when(i == 0)(functools.partial(step, True, False))
    if ni > 1:
        pl.when(jnp.logical_and(i > 0, i < ni))(functools.partial(step, True, True))
    pl.when(i == ni)(functools.partial(step, False, True))


def _proj_deferred(n, w, layer, col0, width, tn, out_dtype, epilogue, extras=()):
    r, d = n.shape
    tm = _tile(r, ROW_TILE)
    assert col0 % tn == 0 and width % tn == 0
    cb0 = col0 // tn
    ni = r // tm

    def mul_tile(i):
        return jnp.minimum(i, ni - 1)

    def epi_tile(i):
        return jnp.maximum(i - 1, 0)

    in_specs = [pl.BlockSpec((tm, d), lambda j, i: (mul_tile(i), 0)),
                pl.BlockSpec((None, d, tn), lambda j, i: (layer, 0, cb0 + j))]
    in_specs += [pl.BlockSpec(shape, functools.partial(lambda imap, j, i: imap(j, epi_tile(i)), imap))
                 for _, shape, imap in extras]
    osize = jnp.dtype(out_dtype).itemsize
    nbytes = (2 * (tm * d * 2 + d * tn * 2 + tm * tn * osize) + 5 * tm * tn * 4
              + sum(2 * 4 * shape[0] * shape[1] for _, shape, _ in extras))
    return pl.pallas_call(
        functools.partial(_proj_deferred_kernel, epilogue=epilogue, ni=ni),
        grid=(width // tn, ni + 1),
        in_specs=in_specs,
        out_specs=pl.BlockSpec((tm, tn), lambda j, i: (epi_tile(i), j)),
        out_shape=jax.ShapeDtypeStruct((r, width), out_dtype),
        scratch_shapes=[pltpu.VMEM((tm, tn), F32)],
        compiler_params=_params(("parallel", "arbitrary"), nbytes),
        name="proj_deferred",
    )(n, w, *[a for a, _, _ in extras])


def _retention_kernel(lg_ref, q_ref, k_ref, v_ref, g_ref, o_ref, bw_ref, *, nc):
    c_len = RET_CHUNK
    h = pl.program_id(1)
    lgf = lg_ref[0, h]
    lgb = lg_ref[1, h]
    ii = lax.broadcasted_iota(jnp.int32, (c_len, c_len), 0).astype(F32)
    jj = lax.broadcasted_iota(jnp.int32, (c_len, c_len), 1).astype(F32)
    diff = ii - jj
    decay = jnp.where(diff >= 0, jnp.exp(jnp.maximum(diff, 0.0) * lgf),
                      jnp.exp(jnp.maximum(-diff, 0.0) * lgb))
    ci = lax.broadcasted_iota(jnp.int32, (c_len, 1), 0).astype(F32)
    xi_f = jnp.exp((ci + 1.0) * lgf)
    zeta_f = jnp.exp((c_len - 1.0 - ci) * lgf)
    xi_b = jnp.exp((c_len - ci) * lgb)
    zeta_b = jnp.exp(ci * lgb)
    one = jnp.ones((1, 1), F32)
    chunk_f = jnp.exp(one * (c_len * lgf))
    chunk_b = jnp.exp(one * (c_len * lgb))
    tn_dims = (((0,), (0,)), ((), ()))
    nt_dims = (((1,), (1,)), ((), ()))

    def rows_of(c):
        return pl.ds(pl.multiple_of(c * c_len, c_len), c_len)

    group = RET_UNROLL if nc % RET_UNROLL == 0 else 1

    def state_chain(state, kvs, chunk_decay):
        before = []
        for kv in kvs:
            before.append(state)
            state = state * chunk_decay + kv
        return before, state

    def bw_body(t, state):
        rows = [rows_of(nc - 1 - (t * group + u)) for u in range(group)]
        kvs = [lax.dot_general((k_ref[r, :].astype(F32) * zeta_b).astype(BF16), v_ref[r, :], tn_dims,
                               preferred_element_type=F32) for r in rows]
        before, state = state_chain(state, kvs, chunk_b)
        for r, s_in in zip(rows, before):
            qx = (q_ref[r, :].astype(F32) * xi_b).astype(BF16)
            bw_ref[r, :] = jnp.dot(qx, s_in.astype(BF16), preferred_element_type=F32)
        return state

    lax.fori_loop(0, nc // group, bw_body, jnp.zeros((HEAD_DIM, HEAD_DIM), F32))

    def fw_body(t, state):
        rows = [rows_of(t * group + u) for u in range(group)]
        scores = [lax.dot_general(q_ref[r, :], k_ref[r, :], nt_dims, preferred_element_type=F32)
                  for r in rows]
        kvs = [lax.dot_general((k_ref[r, :].astype(F32) * zeta_f).astype(BF16), v_ref[r, :], tn_dims,
                               preferred_element_type=F32) for r in rows]
        before, state = state_chain(state, kvs, chunk_f)
        inters = [jnp.dot((q_ref[r, :].astype(F32) * xi_f).astype(BF16), s_in.astype(BF16),
                          preferred_element_type=F32) for r, s_in in zip(rows, before)]
        for r, s, inter in zip(rows, scores, inters):
            intra = jnp.dot((s * decay).astype(BF16), v_ref[r, :], preferred_element_type=F32)
            y = (intra + inter) + bw_ref[r, :]
            y = y * lax.rsqrt(jnp.mean(y * y, axis=-1, keepdims=True) + EPS)
            o_ref[r, :] = (g_ref[r, :] * y).astype(o_ref.dtype)
        return state

    lax.fori_loop(0, nc // group, fw_body, jnp.zeros((HEAD_DIM, HEAD_DIM), F32))


def _retention(qk, v, gate, log_g, row0, batch, seq):
    assert row0 % seq == 0 and seq % RET_CHUNK == 0
    sb0 = row0 // seq
    hd = HEAD_DIM
    nbytes = 2 * seq * hd * (2 + 2 + 2 + 4 + 2) + seq * hd * 4 + 64 * hd * hd * 4
    return pl.pallas_call(
        functools.partial(_retention_kernel, nc=seq // RET_CHUNK),
        grid=(batch, RET_HEADS),
        in_specs=[pl.BlockSpec(memory_space=pltpu.SMEM),
                  pl.BlockSpec((seq, hd), lambda b, h: (sb0 + b, h)),
                  pl.BlockSpec((seq, hd), lambda b, h: (sb0 + b, RET_HEADS + h)),
                  pl.BlockSpec((seq, hd), lambda b, h: (sb0 + b, h)),
                  pl.BlockSpec((seq, hd), lambda b, h: (sb0 + b, h))],
        out_specs=pl.BlockSpec((seq, hd), lambda b, h: (b, h)),
        out_shape=jax.ShapeDtypeStruct((batch * seq, RET_HEADS * hd), BF16),
        scratch_shapes=[pltpu.VMEM((seq, hd), F32)],
        compiler_params=_params(("parallel", "parallel"), nbytes),
        name="retention",
    )(log_g, qk, qk, v, gate)


def _gqa_kernel(q_ref, k_ref, vt_ref, o_ref, qs_ref, m_ref, l_ref, acc_ref, st_ref, *, nk, tq, group, cb):
    hd = HEAD_DIM
    r = group * tq
    for g in range(group):
        qs_ref[g * tq:(g + 1) * tq, :] = q_ref[:, g * hd:(g + 1) * hd]
    m_ref[...] = jnp.full(m_ref.shape, -jnp.inf, F32)
    l_ref[...] = jnp.zeros(l_ref.shape, F32)
    acc_ref[...] = jnp.zeros(acc_ref.shape, F32)
    nt_dims = (((1,), (1,)), ((), ()))

    nb = r // cb

    def scores(c, j):
        return lax.dot_general(k_ref[c], qs_ref[j * cb:(j + 1) * cb, :], nt_dims,
                               preferred_element_type=F32)

    st_ref[...] = scores(0, 0)

    def body(c, carry):
        vt = vt_ref[c]
        st = st_ref[...]
        for j in range(nb):
            cols = slice(j * cb, (j + 1) * cb)
            if j + 1 < nb:
                st_next = scores(c, j + 1)
            else:
                st_next = scores(jnp.minimum(c + 1, nk - 1), 0)
            m_prev = m_ref[:, cols]
            m_new = jnp.maximum(m_prev, jnp.max(st, axis=0, keepdims=True))
            alpha = jnp.exp2(m_prev - m_new)
            p = jnp.exp2(st - m_new)
            l_ref[:, cols] = alpha * l_ref[:, cols] + jnp.sum(p, axis=0, keepdims=True)
            acc_ref[:, cols] = alpha * acc_ref[:, cols] + jnp.dot(vt, p.astype(BF16),
                                                                  preferred_element_type=F32)
            m_ref[:, cols] = m_new
            st = st_next
        st_ref[...] = st
        return carry

    lax.fori_loop(0, nk, body, 0)
    out = (acc_ref[...] / l_ref[...]).T
    for g in range(group):
        o_ref[:, g * hd:(g + 1) * hd] = out[g * tq:(g + 1) * tq, :].astype(o_ref.dtype)


def _gqa(aq, ak, avt, row0, batch, seq):
    hd = HEAD_DIM
    group = ATT_Q_HEADS // ATT_KV_HEADS
    tq = _tile(seq, GQA_Q_ROWS)
    tk = avt.shape[2]
    assert row0 % seq == 0 and seq % tk == 0
    sb0 = row0 // seq
    qb0 = row0 // tq
    nq = seq // tq
    nk = seq // tk
    gw = group * hd
    r = group * tq
    cb = _tile(r, GQA_COL_BLOCK)
    ak3 = ak.reshape(ak.shape[0] // tk, tk, ak.shape[1])
    nbytes = (2 * (2 * tq * gw * 2 + 2 * seq * hd * 2) + r * hd * (2 + 4) + 2 * 8 * r * 4
              + 6 * tk * cb * 4 + 2 * r * hd * 4)
    return pl.pallas_call(
        functools.partial(_gqa_kernel, nk=nk, tq=tq, group=group, cb=cb),
        grid=(batch, ATT_KV_HEADS, nq),
        in_specs=[pl.BlockSpec((tq, gw), lambda b, h, i: (qb0 + b * nq + i, h)),
                  pl.BlockSpec((nk, tk, hd), lambda b, h, i: (sb0 + b, 0, h)),
                  pl.BlockSpec((nk, hd, tk), lambda b, h, i: (sb0 + b, h, 0))],
        out_specs=pl.BlockSpec((tq, gw), lambda b, h, i: (b * nq + i, h)),
        out_shape=jax.ShapeDtypeStruct((batch * seq, ATT_Q_HEADS * hd), BF16),
        scratch_shapes=[pltpu.VMEM((r, hd), BF16),
                        pltpu.VMEM((1, r), F32),
                        pltpu.VMEM((1, r), F32),
                        pltpu.VMEM((hd, r), F32),
                        pltpu.VMEM((tk, cb), F32)],
        compiler_params=_params(("parallel", "parallel", "parallel"), nbytes),
        name="gqa",
    )(aq, ak3, avt)


def _part_specs(parts, tm, cols, row_axis, col_block=None):
    na = parts[0].shape[0] // tm
    assert parts[0].shape[0] % tm == 0 and parts[1].shape[0] % tm == 0

    def spec(first):
        def index_map(*idx):
            i = idx[row_axis]
            row = jnp.minimum(i, na - 1) if first else jnp.maximum(i - na, 0)
            return (row, 0 if col_block is None else col_block(*idx))
        return pl.BlockSpec((tm, cols), index_map)

    return [spec(True), spec(False)], na


def _select_part(i, na, a_ref, b_ref):
    return jnp.where(i < na, a_ref[...], b_ref[...])


def _out_proj_kernel(ra_ref, rb_ref, aa_ref, ab_ref, w_ref, *rest, rw, na):
    i = pl.program_id(1)
    ret = _select_part(i, na, ra_ref, rb_ref)
    att = _select_part(i, na, aa_ref, ab_ref)
    acc = jnp.dot(ret, w_ref[:rw, :], preferred_element_type=F32)
    acc = acc + jnp.dot(att, w_ref[rw:, :], preferred_element_type=F32)
    if len(rest) == 2:
        x_ref, o_ref = rest
        acc = x_ref[...] + acc
    else:
        o_ref, = rest
    o_ref[...] = acc


def _out_proj(ret_parts, att_parts, w, layer, x=None):
    t = ret_parts[0].shape[0] + ret_parts[1].shape[0]
    d = w.shape[2]
    rw, aw = ret_parts[0].shape[1], att_parts[0].shape[1]
    tm = _tile(min(ret_parts[0].shape[0], ret_parts[1].shape[0]), OUT_PROJ_ROWS)
    tn = _tile(d, COL_TILE)
    ret_specs, na = _part_specs(ret_parts, tm, rw, 1)
    att_specs, _ = _part_specs(att_parts, tm, aw, 1)
    in_specs = ret_specs + att_specs + [pl.BlockSpec((None, rw + aw, tn), lambda j, i: (layer, 0, j))]
    operands = [*ret_parts, *att_parts, w]
    if x is not None:
        in_specs.append(pl.BlockSpec((tm, tn), lambda j, i: (i, j)))
        operands.append(x)
    nbytes = (2 * (2 * tm * (rw + aw) * 2 + (rw + aw) * tn * 2 + 2 * tm * tn * 4) + 3 * tm * tn * 4
              + 2 * tm * (rw + aw) * 2)
    return pl.pallas_call(
        functools.partial(_out_proj_kernel, rw=rw, na=na),
        grid=(d // tn, t // tm),
        in_specs=in_specs,
        out_specs=pl.BlockSpec((tm, tn), lambda j, i: (i, j)),
        out_shape=jax.ShapeDtypeStruct((t, d), F32),
        compiler_params=_params(("parallel", "parallel"), nbytes),
        name="out_proj",
    )(*operands)


def _cross_kernel(*refs, na):
    *x_refs, kv_ref, wq_ref, wo_ref, gc_ref, gf_ref, x_out_ref, n_out_ref = refs
    hd = MEM_HEAD_DIM
    mw = MEM_HEADS * hd
    if len(x_refs) == 1:
        x = x_refs[0][...]
    else:
        xa_ref, xb_ref, mix_ref = x_refs
        x = _select_part(pl.program_id(0), na, xa_ref, xb_ref) + mix_ref[...]
    n = _rms(x, gc_ref[...]).astype(BF16)
    q = jnp.dot(n, wq_ref[...], preferred_element_type=F32).astype(BF16)
    scale = hd ** -0.5
    nt_dims = (((1,), (1,)), ((), ()))
    heads = []
    for h in range(MEM_HEADS):
        k = kv_ref[:, h * hd:(h + 1) * hd]
        v = kv_ref[:, mw + h * hd:mw + (h + 1) * hd]
        s = lax.dot_general(q[:, h * hd:(h + 1) * hd], k, nt_dims, preferred_element_type=F32) * scale
        p = jnp.exp(s - jnp.max(s, axis=-1, keepdims=True))
        p = p / jnp.sum(p, axis=-1, keepdims=True)
        heads.append(jnp.dot(p.astype(BF16), v, preferred_element_type=F32).astype(BF16))
    o = jnp.concatenate(heads, axis=-1)
    x2 = x + jnp.dot(o, wo_ref[...], preferred_element_type=F32)
    x_out_ref[...] = x2
    n_out_ref[...] = _rms(x2, gf_ref[...]).astype(n_out_ref.dtype)


def _cross(x, kv, wq, wo, layer, g_cross, g_ffn, mem_block_of_tile, tm, x_parts=None):
    t, d = x.shape
    m = kv.shape[0] // mem_block_of_tile.n_seq
    mw = wq.shape[2]
    x_specs, x_operands, na = [pl.BlockSpec((tm, d), lambda i: (i, 0))], [x], 0
    if x_parts is not None:
        part_specs, na = _part_specs(x_parts, tm, d, 0)
        x_specs, x_operands = part_specs + x_specs, [*x_parts, x]
    nbytes = (2 * (tm * d * (4 * len(x_operands) + 4 + 2) + m * 2 * mw * 2 + 2 * d * mw * 2 + 2 * d * 4)
              + 6 * tm * d * 4)
    return pl.pallas_call(
        functools.partial(_cross_kernel, na=na),
        grid=(t // tm,),
        in_specs=[*x_specs,
                  pl.BlockSpec((m, 2 * mw), lambda i: (mem_block_of_tile(i), 0)),
                  pl.BlockSpec((None, d, mw), lambda i: (layer, 0, 0)),
                  pl.BlockSpec((None, mw, d), lambda i: (layer, 0, 0)),
                  pl.BlockSpec((1, d), lambda i: (0, 0)),
                  pl.BlockSpec((1, d), lambda i: (0, 0))],
        out_specs=[pl.BlockSpec((tm, d), lambda i: (i, 0)),
                   pl.BlockSpec((tm, d), lambda i: (i, 0))],
        out_shape=[jax.ShapeDtypeStruct((t, d), F32), jax.ShapeDtypeStruct((t, d), BF16)],
        compiler_params=_params(("parallel",), nbytes),
        name="cross",
    )(*x_operands, kv, wq, wo, g_cross.reshape(1, d), g_ffn.reshape(1, d))


class _SeqLayout:
    def __init__(self, n_a, len_a, n_b, len_b):
        self.n_a, self.len_a, self.n_b, self.len_b = n_a, len_a, n_b, len_b
        self.rows_a = n_a * len_a
        self.n_seq = n_a + n_b

    def seq_of_row(self, row):
        return jnp.where(row < self.rows_a, row // self.len_a,
                         self.n_a + (row - self.rows_a) // self.len_b)

    def pos_in_seq(self, row):
        return jnp.where(row < self.rows_a, row % self.len_a, (row - self.rows_a) % self.len_b)

    def seq_len_at(self, row):
        return jnp.where(row < self.rows_a, self.len_a, self.len_b)


class _MemBlockOfTile:
    def __init__(self, layout, tm):
        self.layout, self.tm, self.n_seq = layout, tm, layout.n_seq

    def __call__(self, i):
        return self.layout.seq_of_row(i * self.tm)


def _gelu(a):
    return 0.5 * a * (1.0 + lax.erf(a * (2.0 ** -0.5)))


def _ffn_kernel(n_hbm, wa_ref, wu_ref, cwa_ref, cwu_ref, cba_ref, cbu_ref, wd_ref, x_hbm, g_ref,
                out0_hbm, out1_hbm, lhs_ref, h_ref, act_ref, acc_ref, sems,
                *, layout, tm, tf, nf, final, up_chunks, n_tiles):
    outs = (out0_hbm, out1_hbm)
    i = pl.program_id(0)
    f = pl.program_id(1)
    row0 = pl.multiple_of(i * tm, tm)
    first_down = 2
    last_step = nf + 1

    def residual_copy():
        return pltpu.make_async_copy(x_hbm.at[pl.ds(row0, tm), :], acc_ref, sems.at[0])

    def rows_copy(src_row, dst_row, rows, sem):
        return pltpu.make_async_copy(n_hbm.at[pl.ds(src_row, rows), :],
                                     lhs_ref.at[pl.ds(dst_row, rows), :], sems.at[sem])

    slabs = tf // LANES

    def conv(slab, cw, cb):
        lo = h_ref[slab, pl.ds(HALO - 1, tm), :]
        mid = h_ref[slab, pl.ds(HALO, tm), :]
        hi = h_ref[slab, pl.ds(HALO + 1, tm), :]
        return lo * cw[0:1, :] + mid * cw[1:2, :] + hi * cw[2:3, :] + cb

    def gate_into_act():
        for s in range(slabs):
            lanes = slice(s * LANES, (s + 1) * LANES)
            a = conv(s, cwa_ref[:, lanes], cba_ref[:, lanes])
            u = conv(slabs + s, cwu_ref[:, lanes], cbu_ref[:, lanes])
            act_ref[:, lanes] = (_gelu(a) * u).astype(BF16)

    def up_project():
        for r0, r1 in up_chunks:
            lhs = lhs_ref[r0:r1, :]
            ha = jnp.dot(lhs, wa_ref[...], preferred_element_type=F32)
            hu = jnp.dot(lhs, wu_ref[...], preferred_element_type=F32)
            for s in range(slabs):
                lanes = slice(s * LANES, (s + 1) * LANES)
                h_ref[s, r0:r1, :] = ha[:, lanes]
                h_ref[slabs + s, r0:r1, :] = hu[:, lanes]

    def with_acc_drain(tile_row0, action):
        if final:
            ya_hbm, yb_hbm = outs

            @pl.when(tile_row0 < layout.rows_a)
            def _():
                action(pltpu.make_async_copy(acc_ref, ya_hbm.at[pl.ds(tile_row0, tm), :], sems.at[4]))

            @pl.when(tile_row0 >= layout.rows_a)
            def _():
                action(pltpu.make_async_copy(
                    acc_ref, yb_hbm.at[pl.ds(tile_row0 - layout.rows_a, tm), :], sems.at[4]))
        else:
            action(pltpu.make_async_copy(acc_ref, outs[0].at[pl.ds(tile_row0, tm), :], sems.at[4]))

    @pl.when(f == 1)
    def _():
        @pl.when(i > 0)
        def _():
            with_acc_drain(row0 - tm, lambda copy: copy.wait())

        residual_copy().start()

    @pl.when(f == 0)
    def _():
        pos = layout.pos_in_seq(row0)
        first = pos == 0
        last = pos + tm == layout.seq_len_at(row0)
        main = rows_copy(row0, HALO, tm, 1)
        main.start()
        zero = jnp.zeros((HALO, lhs_ref.shape[1]), lhs_ref.dtype)

        @pl.when(first)
        def _():
            lhs_ref[0:HALO, :] = zero

        @pl.when(jnp.logical_not(first))
        def _():
            prev = rows_copy(row0 - HALO, 0, HALO, 2)
            prev.start()
            prev.wait()

        @pl.when(last)
        def _():
            lhs_ref[HALO + tm:, :] = zero

        @pl.when(jnp.logical_not(last))
        def _():
            nxt = rows_copy(row0 + tm, HALO + tm, HALO, 3)
            nxt.start()
            nxt.wait()

        main.wait()

    @pl.when(f == first_down)
    def _():
        residual_copy().wait()

    def stages(down, gate, up):
        if down:
            acc_ref[...] += jnp.dot(act_ref[...], wd_ref[...], preferred_element_type=F32)
        if gate:
            gate_into_act()
        if up:
            up_project()

    pl.when(f == 0)(functools.partial(stages, False, False, True))
    pl.when(f == 1)(functools.partial(stages, False, True, nf > 1))
    if nf > 2:
        pl.when(jnp.logical_and(f >= 2, f < nf))(functools.partial(stages, True, True, True))
    if nf > 1:
        pl.when(f == nf)(functools.partial(stages, True, True, False))
    pl.when(f == last_step)(functools.partial(stages, True, False, False))

    def normalize_rows(store):
        def body(c, carry):
            r0 = pl.multiple_of(c * NORM_CHUNK, NORM_CHUNK)
            store(r0, _rms(acc_ref[pl.ds(r0, NORM_CHUNK), :], g_ref[...]))
            return carry

        lax.fori_loop(0, tm // NORM_CHUNK, body, 0)

    @pl.when(f == last_step)
    def _():
        if final:
            def store(r0, y):
                acc_ref[pl.ds(r0, NORM_CHUNK), :] = y

            normalize_rows(store)
            with_acc_drain(row0, lambda copy: copy.start())
        else:
            with_acc_drain(row0, lambda copy: copy.start())

            def store(r0, y):
                lhs_ref[pl.ds(pl.multiple_of(HALO + r0, HALO), NORM_CHUNK), :] = y.astype(BF16)

            normalize_rows(store)
            out_n = pltpu.make_async_copy(lhs_ref.at[pl.ds(HALO, tm), :],
                                          outs[1].at[pl.ds(row0, tm), :], sems.at[5])
            out_n.start()
            out_n.wait()

        @pl.when(i == n_tiles - 1)
        def _():
            with_acc_drain(row0, lambda copy: copy.wait())


def _ffn(n, x, w_up, conv_w, conv_b, w_down, layer, layout, gain, final):
    t, d = x.shape
    ff = w_down.shape[1]
    tm = _tile(min(layout.len_a, layout.len_b), FFN_ROWS)
    tf = _tile(ff, FFN_COLS)
    assert layout.len_a % tm == 0 and layout.len_b % tm == 0 and tm % HALO == 0
    nf = ff // tf
    conv_b = conv_b.reshape(1, 2 * ff)

    def up_tile(f):
        return jnp.minimum(f, nf - 1)

    def gate_tile(f):
        return jnp.clip(f - 1, 0, nf - 1)

    def down_tile(f):
        return jnp.clip(f - 2, 0, nf - 1)

    hidden_rows = tm + 2 * HALO
    n_chunks = min(FFN_UP_CHUNKS, hidden_rows // HALO)
    bounds = [HALO * ((hidden_rows // HALO) * c // n_chunks) for c in range(n_chunks + 1)]
    up_chunks = tuple(zip(bounds[:-1], bounds[1:]))
    nbytes =(2 * (2 * d * tf * 2 + tf * d * 2) + hidden_rows * (d * 2 + 2 * tf * 4)
              + tm * d * 4 + 4 * hidden_rows * tf * 4 + 3 * tm * tf * 4)
    assert tm % NORM_CHUNK == 0
    if final:
        rows_b = t - layout.rows_a
        out_shape = [jax.ShapeDtypeStruct((layout.rows_a, d), F32), jax.ShapeDtypeStruct((rows_b, d), F32)]
    else:
        out_shape = [jax.ShapeDtypeStruct((t, d), F32), jax.ShapeDtypeStruct((t, d), BF16)]
    return pl.pallas_call(
        functools.partial(_ffn_kernel, layout=layout, tm=tm, tf=tf, nf=nf, final=final,
                          up_chunks=up_chunks, n_tiles=t // tm),
        grid=(t // tm, nf + 2),
        in_specs=[pl.BlockSpec(memory_space=pl.ANY),
                  pl.BlockSpec((None, d, tf), lambda i, f: (layer, 0, up_tile(f))),
                  pl.BlockSpec((None, d, tf), lambda i, f: (layer, 0, nf + up_tile(f))),
                  pl.BlockSpec((3, tf), lambda i, f: (0, gate_tile(f))),
                  pl.BlockSpec((3, tf), lambda i, f: (0, nf + gate_tile(f))),
                  pl.BlockSpec((1, tf), lambda i, f: (0, gate_tile(f))),
                  pl.BlockSpec((1, tf), lambda i, f: (0, nf + gate_tile(f))),
                  pl.BlockSpec((None, tf, d), lambda i, f: (layer, down_tile(f), 0)),
                  pl.BlockSpec(memory_space=pl.ANY),
                  pl.BlockSpec((1, d), lambda i, f: (0, 0))],
        out_specs=[pl.BlockSpec(memory_space=pl.ANY), pl.BlockSpec(memory_space=pl.ANY)],
        out_shape=out_shape,
        scratch_shapes=[pltpu.VMEM((hidden_rows, d), BF16),
                        pltpu.VMEM((2 * tf // LANES, hidden_rows, LANES), F32),
                        pltpu.VMEM((tm, tf), BF16),
                        pltpu.VMEM((tm, d), F32),
                        pltpu.SemaphoreType.DMA((6,))],
        compiler_params=_params(("arbitrary", "arbitrary"), nbytes),
        name="ffn",
    )(n, w_up, w_up, conv_w, conv_w, conv_b, conv_b, w_down, x, gain.reshape(1, d))


def _rope_tables(layout):
    pos = jnp.concatenate([jnp.tile(jnp.arange(layout.len_a), layout.n_a),
                           jnp.tile(jnp.arange(layout.len_b), layout.n_b)])
    r = (pos // GRID_W).astype(F32)
    c = (pos % GRID_W).astype(F32)
    half = HEAD_DIM // 2
    freqs = ROPE_THETA ** (-jnp.arange(0, half, 2, dtype=F32) / half)
    ang_r = r[:, None] * freqs
    ang_c = c[:, None] * freqs
    cos = jnp.concatenate([jnp.cos(ang_r)] * 2 + [jnp.cos(ang_c)] * 2, axis=-1)
    sin = jnp.concatenate([-jnp.sin(ang_r), jnp.sin(ang_r), -jnp.sin(ang_c), jnp.sin(ang_c)], axis=-1)
    return cos, sin


def kernel(x_prompt, x_sample, mem_prompt, mem_sample, w_in, w_out, ret_decay_logit, q_norm, k_norm, norm_mix, norm_cross, norm_mem, w_cq, w_ckv, w_co, norm_ffn, w_up, conv_w, conv_b, w_down, norm_final):
    b, s, d = x_prompt.shape
    db, ds, _ = x_sample.shape
    m = mem_prompt.shape[1]
    depth = w_in.shape[0]
    layout = _SeqLayout(b, s, db, ds)
    t1, t2 = b * s, db * ds
    t = t1 + t2
    rw = RET_HEADS * HEAD_DIM
    aqw = ATT_Q_HEADS * HEAD_DIM
    akw = ATT_KV_HEADS * HEAD_DIM
    assert w_in.shape[2] == 4 * rw + aqw + 2 * akw and rw + aqw == w_out.shape[1]

    x_groups = (x_prompt.reshape(t1, d), x_sample.reshape(t2, d))
    mem =jnp.concatenate([mem_prompt.reshape(b * m, d), mem_sample.reshape(db * m, d)], axis=0)
    cos, sin = _rope_tables(layout)
    log_g = jax.nn.log_sigmoid(ret_decay_logit.astype(F32))

    tm = _tile(t, ROW_TILE)
    tn = _tile(rw, COL_TILE)
    rope_extras = [(cos, (tm, HEAD_DIM), lambda j, i: (i, 0)),
                   (sin, (tm, HEAD_DIM), lambda j, i: (i, 0))]
    gqa_tk = _tile(min(s, ds), GQA_KV_ROWS)
    cross_tm = _tile(min(s, ds), CROSS_ROWS)
    mem_block = _MemBlockOfTile(layout, cross_tm)

    wi, wo = w_in.astype(BF16), w_out.astype(BF16)
    wcq, wckv, wco = w_cq.astype(BF16), w_ckv.astype(BF16), w_co.astype(BF16)
    wup, wdn = w_up.astype(BF16), w_down.astype(BF16)

    n = _rmsnorm_parts(x_groups, norm_mix[0], BF16)
    x = None
    for l in range(depth):
        qk = _proj_deferred(n, wi, l, 0, 2 * rw, tn, BF16,
                            functools.partial(_epi_ret_qk, k_block0=rw // tn, k_scale=HEAD_DIM ** -0.5),
                            rope_extras)
        rv = _proj(n, wi, l, 2 * rw, rw, tn, BF16, _epi_plain)
        gate = _proj(n, wi, l, 3 * rw, rw, tn, F32, _epi_silu)
        aq = _proj_deferred(n, wi, l, 4 * rw, aqw, _tile(aqw, COL_TILE), BF16, _epi_att_q,
                            rope_extras + [(q_norm[l].reshape(1, HEAD_DIM), (1, HEAD_DIM),
                                            lambda j, i: (0, 0))])
        ak = _proj_deferred(n, wi, l, 4 * rw + aqw, akw, akw, BF16, _epi_att_k,
                            rope_extras + [(k_norm[l].reshape(1, HEAD_DIM), (1, HEAD_DIM),
                                            lambda j, i: (0, 0))])
        avt = _proj(n, wi, l, 4 * rw + aqw + akw, akw, akw, BF16, _epi_transposed_chunks,
                    transposed_chunk=gqa_tk)
        ret = (_retention(qk, rv, gate, log_g[l], 0, b, s), _retention(qk, rv, gate, log_g[l], t1, db, ds))
        att = (_gqa(aq, ak, avt, 0, b, s), _gqa(aq, ak, avt, t1, db, ds))
        x = _out_proj(ret, att, wo, l, x)

        memn = _rmsnorm(mem, norm_mem[l], BF16)
        kv = _proj(memn, wckv, l, 0, w_ckv.shape[2], _tile(w_ckv.shape[2], COL_TILE),
                   BF16, _epi_plain, tm_pref=NORM_ROWS)
        x, n = _cross(x, kv, wcq, wco, l, norm_cross[l], norm_ffn[l], mem_block, cross_tm,
                      x_parts=x_groups if l == 0 else None)
        if l + 1 < depth:
            x, n = _ffn(n, x, wup, conv_w[l], conv_b[l], wdn, l, layout, norm_mix[l + 1], final=False)
        else:
            y_a, y_b = _ffn(n, x, wup, conv_w[l], conv_b[l], wdn, l, layout, norm_final, final=True)

    return (y_a.reshape(b, s, d), y_b.reshape(db, ds, d))
```
